```python
import math
import jax, jax.numpy as jnp
from jax import lax
import numpy as np

D_MODEL = 1024
BATCH = 4
SEQ = 8192
DEPTH = 4

N_META = 16
D_MIX = D_MODEL
ATTN_WIDTH = D_MIX // 2
CONV_CH = D_MIX - ATTN_WIDTH
N_ATTN_HEADS = 4
V_HEAD_DIM = ATTN_WIDTH // N_ATTN_HEADS
HEAD_DIM = V_HEAD_DIM // 2
QK_WIDTH = N_ATTN_HEADS * 2 * HEAD_DIM
IN_WIDTH = 2 * QK_WIDTH + ATTN_WIDTH + 2 * CONV_CH
CONV_WIDTH = 31
D_FF = -(-8 * D_MODEL // (3 * 256)) * 256
ROPE_THETA = 10000.0
Q_BLOCK = 128
NORM_EPS = 1e-5

kernel_name = "hybrid_diffattn_conformer_conv_swiglu"


def rmsnorm(x, g):
    x32 = x.astype(jnp.float32)
    y = x32 * lax.rsqrt(jnp.mean(x32 * x32, axis=-1, keepdims=True) + NORM_EPS)
    return (y * g.astype(jnp.float32)).astype(x.dtype)


def layer_norm(x, g, b):
    x32 = x.astype(jnp.float32)
    mu = jnp.mean(x32, axis=-1, keepdims=True)
    xc = x32 - mu
    y = xc * lax.rsqrt(jnp.mean(xc * xc, axis=-1, keepdims=True) + NORM_EPS)
    return (y * g.astype(jnp.float32) + b.astype(jnp.float32)).astype(x.dtype)


def rope_tables(length, dtype):
    pos = jnp.arange(length, dtype=jnp.float32)
    inv = ROPE_THETA ** (-jnp.arange(0, HEAD_DIM, 2, dtype=jnp.float32) / HEAD_DIM)
    ang = pos[:, None] * inv[None, :]
    ang = jnp.concatenate([ang, ang], axis=-1)
    return jnp.cos(ang).astype(dtype), jnp.sin(ang).astype(dtype)


def apply_rope(t, cos, sin):
    c = cos[None, :, None, None, :]
    s = sin[None, :, None, None, :]
    t1, t2 = jnp.split(t, 2, axis=-1)
    return t * c + jnp.concatenate([-t2, t1], axis=-1) * s


def diff_attention(q, k, v, lam, sub_g, lam_init):
    B, L = q.shape[0], q.shape[1]
    n_blk = -(-L // Q_BLOCK)
    L_pad = n_blk * Q_BLOCK
    pad = L_pad - L
    q = jnp.pad(q, ((0, 0), (0, pad), (0, 0), (0, 0), (0, 0)))
    k = jnp.pad(k, ((0, 0), (0, pad), (0, 0), (0, 0), (0, 0)))
    v = jnp.pad(v, ((0, 0), (0, pad), (0, 0), (0, 0)))
    qb = q.reshape(B, n_blk, Q_BLOCK, N_ATTN_HEADS, 2, HEAD_DIM).transpose(1, 0, 2, 3, 4, 5)
    k_pos = jnp.arange(L_pad)
    scale = HEAD_DIM ** -0.5

    def one_block(args):
        q_blk, start = args
        s = jnp.einsum('bqhcd,bkhcd->bhcqk', q_blk, k).astype(jnp.float32) * scale
        q_pos = start + jnp.arange(Q_BLOCK)
        mask = k_pos[None, :] <= q_pos[:, None]
        s = jnp.where(mask, s, -jnp.inf)
        p = jax.nn.softmax(s, axis=-1)
        a = p[:, :, 0] - lam * p[:, :, 1]
        return jnp.einsum('bhqk,bkhe->bqhe', a.astype(v.dtype), v)

    starts = jnp.arange(n_blk) * Q_BLOCK
    o = lax.map(one_block, (qb, starts))
    o = o.transpose(1, 0, 2, 3, 4).reshape(B, L_pad, N_ATTN_HEADS, V_HEAD_DIM)[:, :L]
    o = rmsnorm(o, sub_g) * (1.0 - lam_init)
    return o.reshape(B, L, ATTN_WIDTH)


def conformer_conv(u, conv_w, conv_b, ln_g, ln_b):
    a, gate = jnp.split(u, 2, axis=-1)
    z = a * jax.nn.sigmoid(gate)
    z = lax.conv_general_dilated(
        z, conv_w[:, None, :], window_strides=(1,), padding=((CONV_WIDTH - 1, 0),),
        dimension_numbers=('NWC', 'WIO', 'NWC'), feature_group_count=CONV_CH)
    z = z + conv_b
    z = layer_norm(z, ln_g, ln_b)
    return jax.nn.silu(z)


def setup_inputs(seed: int = 0) -> dict:
    key = jax.random.key(seed)
    ks = jax.random.split(key, 20)
    f = jnp.float32
    nrm = lambda k, shape, s: jax.random.normal(k, shape, f) * s
    return {
        "x": nrm(ks[0], (BATCH, SEQ, D_MODEL), 1.0),
        "meta_tokens": nrm(ks[1], (N_META, D_MODEL), 1.0),
        "norm1_g": 1.0 + nrm(ks[2], (DEPTH, D_MODEL), 0.02),
        "w_in": nrm(ks[3], (DEPTH, D_MODEL, IN_WIDTH), D_MODEL ** -0.5),
        "b_glu": nrm(ks[4], (DEPTH, 2 * CONV_CH), 0.02),
        "conv_w": nrm(ks[5], (DEPTH, CONV_WIDTH, CONV_CH), CONV_WIDTH ** -0.5),
        "conv_b": nrm(ks[6], (DEPTH, CONV_CH), 0.02),
        "conv_ln_g": 1.0 + nrm(ks[7], (DEPTH, CONV_CH), 0.02),
        "conv_ln_b": nrm(ks[8], (DEPTH, CONV_CH), 0.02),
        "lam_q1": nrm(ks[9], (DEPTH, HEAD_DIM), 0.1),
        "lam_k1": nrm(ks[10], (DEPTH, HEAD_DIM), 0.1),
        "lam_q2": nrm(ks[11], (DEPTH, HEAD_DIM), 0.1),
        "lam_k2": nrm(ks[12], (DEPTH, HEAD_DIM), 0.1),
        "subln_g": 1.0 + nrm(ks[13], (DEPTH, V_HEAD_DIM), 0.02),
        "w_out": nrm(ks[14], (DEPTH, D_MIX, D_MODEL), D_MIX ** -0.5),
        "norm2_g": 1.0 + nrm(ks[15], (DEPTH, D_MODEL), 0.02),
        "w_gate_up": nrm(ks[16], (DEPTH, D_MODEL, 2 * D_FF), D_MODEL ** -0.5),
        "w_down": nrm(ks[17], (DEPTH, D_FF, D_MODEL), D_FF ** -0.5),
        "final_g": 1.0 + nrm(ks[18], (D_MODEL,), 0.02),
    }


def reference(x, meta_tokens, norm1_g, w_in, b_glu, conv_w, conv_b, conv_ln_g, conv_ln_b,
              lam_q1, lam_k1, lam_q2, lam_k2, subln_g, w_out, norm2_g, w_gate_up, w_down,
              final_g):
    B, S, D = x.shape
    meta = jnp.broadcast_to(meta_tokens[None].astype(x.dtype), (B, N_META, D))
    h = jnp.concatenate([meta, x], axis=1)
    L = S + N_META
    cos, sin = rope_tables(L, x.dtype)
    for l in range(DEPTH):
        lam_init = 0.8 - 0.6 * math.exp(-0.3 * l)
        hn = rmsnorm(h, norm1_g[l])
        proj = hn @ w_in[l]
        q, k, v, u = jnp.split(proj, [QK_WIDTH, 2 * QK_WIDTH, 2 * QK_WIDTH + ATTN_WIDTH], axis=-1)
        q = apply_rope(q.reshape(B, L, N_ATTN_HEADS, 2, HEAD_DIM), cos, sin)
        k = apply_rope(k.reshape(B, L, N_ATTN_HEADS, 2, HEAD_DIM), cos, sin)
        v = v.reshape(B, L, N_ATTN_HEADS, V_HEAD_DIM)
        lam = (jnp.exp(jnp.sum(lam_q1[l].astype(jnp.float32) * lam_k1[l].astype(jnp.float32)))
               - jnp.exp(jnp.sum(lam_q2[l].astype(jnp.float32) * lam_k2[l].astype(jnp.float32)))
               + lam_init)
        attn_out = diff_attention(q, k, v, lam, subln_g[l], lam_init)
        conv_out = conformer_conv(u + b_glu[l], conv_w[l], conv_b[l],
                                  conv_ln_g[l], conv_ln_b[l])
        h = h + jnp.concatenate([attn_out, conv_out], axis=-1) @ w_out[l]
        hn = rmsnorm(h, norm2_g[l])
        gate, up = jnp.split(hn @ w_gate_up[l], 2, axis=-1)
        h = h + (jax.nn.silu(gate) * up) @ w_down[l]
    h = rmsnorm(h, final_g)
    return h[:, N_META:]
```

```python
import functools
import math

import jax
import jax.numpy as jnp
from jax import lax
from jax.experimental import pallas as pl
from jax.experimental.pallas import tpu as pltpu

D_MODEL = 1024
N_META = 16
ATTN_WIDTH = 512
CONV_CH = 512
N_HEADS = 4
V_HEAD_DIM = 128
HEAD_DIM = 64
QK_WIDTH = 512
IN_WIDTH = 2 * QK_WIDTH + ATTN_WIDTH + 2 * CONV_CH
CONV_WIDTH = 31
D_FF = 2816
ROPE_THETA = 10000.0
NORM_EPS = 1e-5

HALO = 32
TOKEN_TILE = 512
Q_TILE = 256
FF_CHUNK = 256
VMEM_LIMIT = 56 * 1024 * 1024

F32 = jnp.float32
BF16 = jnp.bfloat16


def _dot(a, b):
    return jnp.dot(a, b, preferred_element_type=F32)


def _dot_nt(a, b):
    return lax.dot_general(a, b, (((1,), (1,)), ((), ())), preferred_element_type=F32)


def _rms(x, g):
    ms = jnp.mean(x * x, axis=-1, keepdims=True)
    return x * lax.rsqrt(ms + NORM_EPS) * g


def _resident(shape):
    zeros = (0,) * len(shape)
    return pl.BlockSpec(shape, lambda *_: zeros, pipeline_mode=pl.Buffered(1))


def _proj_kernel(h_ref, g_ref, w_ref, b_ref, cos_ref, sa_ref, sb_ref,
                 q_ref, k_ref, v_ref, z_ref):
    hn = _rms(h_ref[...], g_ref[...]).astype(BF16)
    cos = cos_ref[...]
    sin_lo = sa_ref[...]
    sin_hi = sb_ref[...]

    def rope(t):
        return t * cos + pltpu.roll(t, 96, 1) * sin_lo + pltpu.roll(t, 32, 1) * sin_hi

    qf = _dot(hn, w_ref[:, 0:QK_WIDTH])
    for g in range(N_HEADS):
        sl = slice(g * 128, (g + 1) * 128)
        q_ref[:, sl] = (rope(qf[:, sl]) * (HEAD_DIM ** -0.5)).astype(BF16)
    kf = _dot(hn, w_ref[:, QK_WIDTH:2 * QK_WIDTH])
    for g in range(N_HEADS):
        sl = slice(g * 128, (g + 1) * 128)
        k_ref[:, sl] = rope(kf[:, sl]).astype(BF16)
    v_ref[...] = _dot(hn, w_ref[:, 2 * QK_WIDTH:2 * QK_WIDTH + ATTN_WIDTH]).astype(BF16)
    u = _dot(hn, w_ref[:, 2 * QK_WIDTH + ATTN_WIDTH:IN_WIDTH]) + b_ref[...]
    z_ref[...] = u[:, :CONV_CH] * jax.nn.sigmoid(u[:, CONV_CH:])


def _proj_call(h, g, w, b, cos, sin_lo, sin_hi, tm, tiles_per_seq, name):
    t = h.shape[0]
    row = lambda i: (i, 0)
    pos = lambda i: (i % tiles_per_seq, 0)
    half = pl.BlockSpec((tm, 512), row)
    tab = pl.BlockSpec((tm, 128), pos)
    return pl.pallas_call(
        _proj_kernel,
        grid=(t // tm,),
        in_specs=[pl.BlockSpec((tm, D_MODEL), row), _resident((1, D_MODEL)),
                  _resident((D_MODEL, IN_WIDTH)), _resident((1, 2 * CONV_CH)),
                  tab, tab, tab],
        out_specs=[half, half, half, half],
        out_shape=[jax.ShapeDtypeStruct((t, 512), BF16)] * 3
        + [jax.ShapeDtypeStruct((t, 512), F32)],
        compiler_params=pltpu.CompilerParams(
            dimension_semantics=("parallel",), vmem_limit_bytes=VMEM_LIMIT),
        name=name,
    )(h, g, w, b, cos, sin_lo, sin_hi)


def _stack_components(q):
    lane = lax.broadcasted_iota(jnp.int32, q.shape, 1)
    zero = jnp.zeros_like(q)
    return jnp.concatenate([jnp.where(lane < HEAD_DIM, q, zero),
                            jnp.where(lane >= HEAD_DIM, q, zero)], axis=0)


def _flash_init(m_scr, l_scr, acc_scr):
    m_scr[...] = jnp.full(m_scr.shape, -jnp.inf, F32)
    l_scr[...] = jnp.zeros(l_scr.shape, F32)
    acc_scr[...] = jnp.zeros(acc_scr.shape, F32)


def _flash_step(qs, kj, vj, m_scr, l_scr, acc_scr, mask=None):
    s = _dot_nt(qs, kj)
    if mask is not None:
        s = jnp.where(mask, s, -jnp.inf)
    m_prev = m_scr[...]
    m_new = jnp.maximum(m_prev, jnp.max(s, axis=-1, keepdims=True))
    alpha = jnp.exp(m_prev - m_new)
    p = jnp.exp(s - m_new)
    l_scr[...] = alpha * l_scr[...] + jnp.sum(p, axis=-1, keepdims=True)
    acc_scr[...] = alpha * acc_scr[...] + _dot(p.astype(BF16), vj)
    m_scr[...] = m_new


def _causal_mask(n, nk):
    r = lax.broadcasted_iota(jnp.int32, (n, nk), 0)
    c = lax.broadcasted_iota(jnp.int32, (n, nk), 1)
    m = c <= r
    return jnp.concatenate([m, m], axis=0)


def _flash_finish(lam_ref, subg_ref, l_scr, acc_scr, n, lam_init):
    lq1, lk1, lq2, lk2 = (lam_ref[i:i + 1, :] for i in range(4))
    lam = (jnp.exp(jnp.sum(lq1 * lk1, axis=-1, keepdims=True))
           - jnp.exp(jnp.sum(lq2 * lk2, axis=-1, keepdims=True)) + lam_init)
    o = acc_scr[0:n, :] / l_scr[0:n, :] - lam * (acc_scr[n:2 * n, :] / l_scr[n:2 * n, :])
    return (_rms(o, subg_ref[...]) * (1.0 - lam_init)).astype(BF16)


def _attn_kernel(lam_ref, subg_ref, q_ref, k_ref, v_ref, km_ref, vm_ref, o_ref,
                 m_scr, l_scr, acc_scr, *, lam_init, tq):
    i = pl.program_id(2)
    qs = _stack_components(q_ref[...])
    _flash_init(m_scr, l_scr, acc_scr)
    _flash_step(qs, km_ref[...], vm_ref[...], m_scr, l_scr, acc_scr)

    def body(j, carry):
        off = pl.multiple_of(j * tq, tq)
        _flash_step(qs, k_ref[pl.ds(off, tq), :], v_ref[pl.ds(off, tq), :],
                    m_scr, l_scr, acc_scr)
        return carry

    lax.fori_loop(0, i, body, 0)
    off = pl.multiple_of(i * tq, tq)
    _flash_step(qs, k_ref[pl.ds(off, tq), :], v_ref[pl.ds(off, tq), :],
                m_scr, l_scr, acc_scr, mask=_causal_mask(tq, tq))
    o_ref[...] = _flash_finish(lam_ref, subg_ref, l_scr, acc_scr, tq, lam_init)


def _attn_call(lam, subg, q, k, v, km, vm, lam_init, name):
    b, s, _ = q.shape
    tq = Q_TILE
    qblk = pl.BlockSpec((None, tq, 128), lambda bi, hi, qi: (bi, qi, hi))
    kvblk = pl.BlockSpec((None, s, 128), lambda bi, hi, qi: (bi, 0, hi))
    mblk = pl.BlockSpec((N_META, 128), lambda bi, hi, qi: (0, hi))
    return pl.pallas_call(
        functools.partial(_attn_kernel, lam_init=lam_init, tq=tq),
        grid=(b, N_HEADS, s // tq),
        in_specs=[_resident((4, HEAD_DIM)), _resident((1, V_HEAD_DIM)),
                  qblk, kvblk, kvblk, mblk, mblk],
        out_specs=qblk,
        out_shape=jax.ShapeDtypeStruct((b, s, ATTN_WIDTH), BF16),
        scratch_shapes=[pltpu.VMEM((2 * tq, 1), F32), pltpu.VMEM((2 * tq, 1), F32),
                        pltpu.VMEM((2 * tq, V_HEAD_DIM), F32)],
        compiler_params=pltpu.CompilerParams(
            dimension_semantics=("parallel", "parallel", "arbitrary"),
            vmem_limit_bytes=VMEM_LIMIT),
        name=name,
    )(lam, subg, q, k, v, km, vm)


def _meta_attn_kernel(lam_ref, subg_ref, q_ref, k_ref, v_ref, o_ref,
                      m_scr, l_scr, acc_scr, *, lam_init):
    mask = _causal_mask(N_META, N_META)
    for h in range(N_HEADS):
        sl = slice(h * 128, (h + 1) * 128)
        _flash_init(m_scr, l_scr, acc_scr)
        _flash_step(_stack_components(q_ref[:, sl]), k_ref[:, sl], v_ref[:, sl],
                    m_scr, l_scr, acc_scr, mask=mask)
        o_ref[:, sl] = _flash_finish(lam_ref, subg_ref, l_scr, acc_scr, N_META, lam_init)


def _meta_attn_call(lam, subg, q, k, v, lam_init, name):
    return pl.pallas_call(
        functools.partial(_meta_attn_kernel, lam_init=lam_init),
        out_shape=jax.ShapeDtypeStruct((N_META, ATTN_WIDTH), BF16),
        scratch_shapes=[pltpu.VMEM((2 * N_META, 1), F32), pltpu.VMEM((2 * N_META, 1), F32),
                        pltpu.VMEM((2 * N_META, V_HEAD_DIM), F32)],
        name=name,
    )(lam, subg, q, k, v)


def _conv_kernel(z_ref, halo_ref, prefix_ref, w_ref, b_ref, g_ref, beta_ref, o_ref,
                 ext_scr, *, tm, rows, tiles_per_seq):
    i = pl.program_id(0)
    ext_scr[0:HALO, :] = jnp.where(i % tiles_per_seq == 0, prefix_ref[...], halo_ref[...])
    ext_scr[HALO:HALO + tm, :] = z_ref[...]
    first = HALO - (CONV_WIDTH - 1)
    for c in range(tm // rows):
        acc = jnp.zeros((rows, CONV_CH), F32)
        for j in range(CONV_WIDTH):
            acc = acc + w_ref[j:j + 1, :] * ext_scr[pl.ds(first + j + c * rows, rows), :]
        y = acc + b_ref[...]
        yc = y - jnp.mean(y, axis=-1, keepdims=True)
        var = jnp.mean(yc * yc, axis=-1, keepdims=True)
        y = yc * lax.rsqrt(var + NORM_EPS) * g_ref[...] + beta_ref[...]
        o_ref[c * rows:(c + 1) * rows, :] = (y * jax.nn.sigmoid(y)).astype(BF16)


def _conv_call(z, halo_src, prefix, w, b, g, beta, tm, tiles_per_seq, name):
    t = z.shape[0]
    row = lambda i: (i, 0)
    halo = lambda i: (jnp.maximum(i * (tm // HALO) - 1, 0), 0)
    rows = min(tm, 64)
    return pl.pallas_call(
        functools.partial(_conv_kernel, tm=tm, rows=rows, tiles_per_seq=tiles_per_seq),
        grid=(t // tm,),
        in_specs=[pl.BlockSpec((tm, CONV_CH), row), pl.BlockSpec((HALO, CONV_CH), halo),
                  _resident((HALO, CONV_CH)), _resident((CONV_WIDTH, CONV_CH)),
                  _resident((1, CONV_CH)), _resident((1, CONV_CH)), _resident((1, CONV_CH))],
        out_specs=pl.BlockSpec((tm, CONV_CH), row),
        out_shape=jax.ShapeDtypeStruct((t, CONV_CH), BF16),
        scratch_shapes=[pltpu.VMEM((HALO + tm, CONV_CH), F32)],
        compiler_params=pltpu.CompilerParams(
            dimension_semantics=("parallel",), vmem_limit_bytes=VMEM_LIMIT),
        name=name,
    )(z, halo_src, prefix, w, b, g, beta)


def _ffn_kernel(h_ref, a_ref, c_ref, wo_ref, g2_ref, wgu_ref, wd_ref, fg_ref, o_ref,
                hn_scr, act_scr, *, final):
    h1 = (h_ref[...] + _dot(a_ref[...], wo_ref[0:ATTN_WIDTH, :])
          + _dot(c_ref[...], wo_ref[ATTN_WIDTH:ATTN_WIDTH + CONV_CH, :]))
    o_ref[...] = h1
    hn_scr[...] = _rms(h1, g2_ref[...]).astype(BF16)
    for c in range(D_FF // FF_CHUNK):
        sl = slice(c * FF_CHUNK, (c + 1) * FF_CHUNK)
        hn = hn_scr[...]
        gate = _dot(hn, wgu_ref[:, sl])
        up = _dot(hn, wgu_ref[:, D_FF + c * FF_CHUNK:D_FF + (c + 1) * FF_CHUNK])
        act_scr[:, sl] = (gate * jax.nn.sigmoid(gate) * up).astype(BF16)
    h2 = o_ref[...] + _dot(act_scr[...], wd_ref[...])
    if final:
        h2 = _rms(h2, fg_ref[...])
    o_ref[...] = h2


def _ffn_call(h, a, c, wo, g2, wgu, wd, fg, tm, final, name):
    t = h.shape[0]
    row = lambda i: (i, 0)
    return pl.pallas_call(
        functools.partial(_ffn_kernel, final=final),
        grid=(t // tm,),
        in_specs=[pl.BlockSpec((tm, D_MODEL), row), pl.BlockSpec((tm, ATTN_WIDTH), row),
                  pl.BlockSpec((tm, CONV_CH), row), _resident((D_MODEL, D_MODEL)),
                  _resident((1, D_MODEL)), _resident((D_MODEL, 2 * D_FF)),
                  _resident((D_FF, D_MODEL)), _resident((1, D_MODEL))],
        out_specs=pl.BlockSpec((tm, D_MODEL), row),
        out_shape=jax.ShapeDtypeStruct((t, D_MODEL), F32),
        scratch_shapes=[pltpu.VMEM((tm, D_MODEL), BF16), pltpu.VMEM((tm, D_FF), BF16)],
        compiler_params=pltpu.CompilerParams(
            dimension_semantics=("parallel",), vmem_limit_bytes=VMEM_LIMIT),
        name=name,
    )(h, a, c, wo, g2, wgu, wd, fg)


def _rope_tables(length):
    pos = jnp.arange(length, dtype=F32)
    inv = ROPE_THETA ** (-jnp.arange(0, HEAD_DIM, 2, dtype=F32) / HEAD_DIM)
    ang = pos[:, None] * inv[None, :]
    cos = jnp.tile(jnp.cos(ang), (1, 4))
    sin = jnp.tile(jnp.sin(ang), (1, 4))
    first_half = (jnp.arange(128) % HEAD_DIM) < HEAD_DIM // 2
    return cos, jnp.where(first_half, -sin, 0.0), jnp.where(first_half, 0.0, sin)


def kernel(x, meta_tokens, norm1_g, w_in, b_glu, conv_w, conv_b, conv_ln_g, conv_ln_b,
           lam_q1, lam_k1, lam_q2, lam_k2, subln_g, w_out, norm2_g, w_gate_up, w_down,
           final_g):
    bsz, seq, d = x.shape
    depth = w_in.shape[0]
    tm = TOKEN_TILE
    tps = seq // tm
    h = x.reshape(bsz * seq, d)
    hm = meta_tokens.astype(x.dtype)

    tables = _rope_tables(N_META + seq)
    tab_meta = [t[:N_META] for t in tables]
    tab_main = [t[N_META:] for t in tables]

    w_in_b = w_in.astype(BF16)
    w_out_b = w_out.astype(BF16)
    w_gu_b = w_gate_up.astype(BF16)
    w_dn_b = w_down.astype(BF16)
    row = lambda a: a.reshape(1, -1)
    zero_halo = jnp.zeros((HALO, CONV_CH), F32)
    fg = row(final_g)

    for l in range(depth):
        lam_init = 0.8 - 0.6 * math.exp(-0.3 * l)
        last = l == depth - 1
        lam = jnp.stack([lam_q1[l], lam_k1[l], lam_q2[l], lam_k2[l]]).astype(F32)
        subg = row(subln_g[l])
        pargs = (row(norm1_g[l]), w_in_b[l], row(b_glu[l]))
        q, k, v, z = _proj_call(h, *pargs, *tab_main, tm, tps, f"proj{l}")
        qm, km, vm, zm = _proj_call(hm, *pargs, *tab_meta, N_META, 1, f"proj_meta{l}")

        attn = _attn_call(lam, subg, q.reshape(bsz, seq, -1), k.reshape(bsz, seq, -1),
                          v.reshape(bsz, seq, -1), km, vm, lam_init, f"attn{l}")
        attn = attn.reshape(bsz * seq, -1)

        cargs = (conv_w[l], row(conv_b[l]), row(conv_ln_g[l]), row(conv_ln_b[l]))
        prefix = jnp.concatenate([jnp.zeros((HALO - N_META, CONV_CH), F32), zm], axis=0)
        conv = _conv_call(z, z, prefix, *cargs, tm, tps, f"conv{l}")

        fargs = (w_out_b[l], row(norm2_g[l]), w_gu_b[l], w_dn_b[l], fg)
        h = _ffn_call(h, attn, conv, *fargs, tm, last, f"ffn{l}")
        if not last:
            attn_m = _meta_attn_call(lam, subg, qm, km, vm, lam_init, f"attn_meta{l}")
            conv_m = _conv_call(zm, zero_halo, zero_halo, *cargs, N_META, 1, f"conv_meta{l}")
            hm = _ffn_call(hm, attn_m, conv_m, *fargs, N_META, False, f"ffn_meta{l}")
    return h.reshape(bsz, seq, d)
```

```python
import functools
import math

import jax
import jax.numpy as jnp
from jax import lax
from jax.experimental import pallas as pl
from jax.experimental.pallas import tpu as pltpu

D_MODEL = 1024
N_META = 16
ATTN_WIDTH = 512
CONV_CH = 512
N_HEADS = 4
V_HEAD_DIM = 128
HEAD_DIM = 64
QK_WIDTH = 512
IN_WIDTH = 2 * QK_WIDTH + ATTN_WIDTH + 2 * CONV_CH
CONV_WIDTH = 31
D_FF = 2816
ROPE_THETA = 10000.0
NORM_EPS = 1e-5
LOG2E = math.log2(math.e)
SUBLANES = 8

HALO = 32
META_PAD = 128
TOKEN_TILE = 512
Q_TILE = 256
K_TILE = 512
FF_CHUNK = 256
VMEM_LIMIT = 56 * 1024 * 1024

F32 = jnp.float32
BF16 = jnp.bfloat16


def _dot(a, b):
    return jnp.dot(a, b, preferred_element_type=F32)


def _dot_nt(a, b):
    return lax.dot_general(a, b, (((1,), (1,)), ((), ())), preferred_element_type=F32)


def _rms(x, g):
    ms = jnp.mean(x * x, axis=-1, keepdims=True)
    return x * lax.rsqrt(ms + NORM_EPS) * g


def _resident(shape):
    zeros = (0,) * len(shape)
    return pl.BlockSpec(shape, lambda *_: zeros, pipeline_mode=pl.Buffered(1))


def _proj_kernel(h_ref, g_ref, w_ref, b_ref, cos_ref, sa_ref, sb_ref,
                 q_ref, k_ref, v_ref, z_ref):
    hn = _rms(h_ref[...], g_ref[...]).astype(BF16)
    cos = cos_ref[...]
    sin_lo = sa_ref[...]
    sin_hi = sb_ref[...]

    def rope(t):
        return t * cos + pltpu.roll(t, 96, 1) * sin_lo + pltpu.roll(t, 32, 1) * sin_hi

    qf = _dot(hn, w_ref[:, 0:QK_WIDTH])
    for g in range(N_HEADS):
        sl = slice(g * 128, (g + 1) * 128)
        q_ref[:, sl] = (rope(qf[:, sl]) * (HEAD_DIM ** -0.5 * LOG2E)).astype(BF16)
    kf = _dot(hn, w_ref[:, QK_WIDTH:2 * QK_WIDTH])
    for g in range(N_HEADS):
        sl = slice(g * 128, (g + 1) * 128)
        k_ref[:, sl] = rope(kf[:, sl]).astype(BF16)
    v_ref[...] = _dot(hn, w_ref[:, 2 * QK_WIDTH:2 * QK_WIDTH + ATTN_WIDTH]).astype(BF16)
    u = _dot(hn, w_ref[:, 2 * QK_WIDTH + ATTN_WIDTH:IN_WIDTH]) + b_ref[...]
    z_ref[...] = u[:, :CONV_CH] * jax.nn.sigmoid(u[:, CONV_CH:])


def _proj_call(h, g, w, b, cos, sin_lo, sin_hi, tm, tiles_per_seq, name):
    t = h.shape[0]
    row = lambda i: (i, 0)
    pos = lambda i: (i % tiles_per_seq, 0)
    half = pl.BlockSpec((tm, 512), row)
    tab = pl.BlockSpec((tm, 128), pos)
    return pl.pallas_call(
        _proj_kernel,
        grid=(t // tm,),
        in_specs=[pl.BlockSpec((tm, D_MODEL), row), _resident((1, D_MODEL)),
                  _resident((D_MODEL, IN_WIDTH)), _resident((1, 2 * CONV_CH)),
                  tab, tab, tab],
        out_specs=[half, half, half, half],
        out_shape=[jax.ShapeDtypeStruct((t, 512), BF16)] * 3
        + [jax.ShapeDtypeStruct((t, 512), F32)],
        compiler_params=pltpu.CompilerParams(
            dimension_semantics=("parallel",), vmem_limit_bytes=VMEM_LIMIT),
        name=name,
    )(h, g, w, b, cos, sin_lo, sin_hi)


def _stack_components(q):
    lane = lax.broadcasted_iota(jnp.int32, q.shape, 1)
    zero = jnp.zeros_like(q)
    return jnp.concatenate([jnp.where(lane < HEAD_DIM, q, zero),
                            jnp.where(lane >= HEAD_DIM, q, zero)], axis=0)


def _flash_init(m_ref, l_ref, acc_ref):
    m_ref[...] = jnp.full(m_ref.shape, -jnp.inf, F32)
    l_ref[...] = jnp.zeros(l_ref.shape, F32)
    acc_ref[...] = jnp.zeros(acc_ref.shape, F32)


def _flash_step(qs, kj, vj, m_ref, l_ref, acc_ref, mask=None):
    reps = kj.shape[0] // 128
    s = _dot_nt(qs, kj)
    if mask is not None:
        s = jnp.where(mask, s, -jnp.inf)
    m_prev = m_ref[...]
    m_new = jnp.maximum(m_prev, jnp.max(s, axis=-1, keepdims=True))
    alpha = jnp.exp2(m_prev - m_new)
    p = jnp.exp2(s - jnp.concatenate([m_new] * reps, axis=1))
    psum = p[:, 0:128]
    for t in range(1, reps):
        psum = psum + p[:, t * 128:(t + 1) * 128]
    l_ref[...] = alpha * l_ref[...] + psum
    acc_ref[...] = alpha * acc_ref[...] + _dot(p.astype(BF16), vj)
    m_ref[...] = m_new


def _causal_mask(n, nk, shift):
    r = lax.broadcasted_iota(jnp.int32, (2 * n, nk), 0) & (n - 1)
    c = lax.broadcasted_iota(jnp.int32, (2 * n, nk), 1)
    return c - r <= shift


def _flash_finish(lam_ref, subg_ref, l_ref, acc_ref, n):
    lq1, lk1, lq2, lk2 = (lam_ref[i:i + 1, :] for i in range(4))
    lam_init = lam_ref[4:5, 0:1]
    lam = (jnp.exp(jnp.sum(lq1 * lk1, axis=-1, keepdims=True))
           - jnp.exp(jnp.sum(lq2 * lk2, axis=-1, keepdims=True)) + lam_init)
    l = jnp.sum(l_ref[...], axis=-1, keepdims=True)
    o = acc_ref[0:n, :] / l[0:n] - lam * (acc_ref[n:2 * n, :] / l[n:2 * n])
    return (_rms(o, subg_ref[...]) * (1.0 - lam_init)).astype(BF16)


def _attn_kernel(lam_ref, subg_ref, q_ref, k_ref, v_ref, km_ref, vm_ref, o_ref,
                 qs_scr, m_scr, l_scr, acc_scr, *, tq, tk):
    i = pl.program_id(1)
    heads = [slice(h * 128, (h + 1) * 128) for h in range(N_HEADS)]
    stats = [(m_scr.at[h], l_scr.at[h], acc_scr.at[h]) for h in range(N_HEADS)]
    meta_mask = lax.broadcasted_iota(jnp.int32, (2 * tq, META_PAD), 1) < N_META
    for h in range(N_HEADS):
        qs_scr[h] = _stack_components(q_ref[:, heads[h]])
        _flash_init(*stats[h])
        _flash_step(qs_scr[h], km_ref[:, heads[h]], vm_ref[:, heads[h]], *stats[h],
                    mask=meta_mask)

    def kv_block(j, mask):
        off = pl.multiple_of(j * tk, tk)
        for h in range(N_HEADS):
            _flash_step(qs_scr[h], k_ref[pl.ds(off, tk), heads[h]],
                        v_ref[pl.ds(off, tk), heads[h]], *stats[h], mask=mask)

    n_full = (i * tq) // tk

    def body(j, carry):
        kv_block(j, None)
        return carry

    lax.fori_loop(0, n_full, body, 0)
    kv_block(n_full, _causal_mask(tq, tk, i * tq - n_full * tk))
    for h in range(N_HEADS):
        o_ref[:, heads[h]] = _flash_finish(lam_ref, subg_ref, stats[h][1], stats[h][2], tq)


def _attn_call(lam, subg, q, k, v, km, vm, name):
    b, s, _ = q.shape
    tq, tk = Q_TILE, K_TILE
    qblk = pl.BlockSpec((None, tq, ATTN_WIDTH), lambda bi, qi: (bi, qi, 0))
    kvblk = pl.BlockSpec((None, s, ATTN_WIDTH), lambda bi, qi: (bi, 0, 0),
                         pipeline_mode=pl.Buffered(1))
    return pl.pallas_call(
        functools.partial(_attn_kernel, tq=tq, tk=tk),
        grid=(b, s // tq),
        in_specs=[_resident((8, HEAD_DIM)), _resident((1, V_HEAD_DIM)),
                  qblk, kvblk, kvblk,
                  _resident((META_PAD, ATTN_WIDTH)), _resident((META_PAD, ATTN_WIDTH))],
        out_specs=qblk,
        out_shape=jax.ShapeDtypeStruct((b, s, ATTN_WIDTH), BF16),
        scratch_shapes=[pltpu.VMEM((N_HEADS, 2 * tq, 128), BF16),
                        pltpu.VMEM((N_HEADS, 2 * tq, 128), F32),
                        pltpu.VMEM((N_HEADS, 2 * tq, 128), F32),
                        pltpu.VMEM((N_HEADS, 2 * tq, V_HEAD_DIM), F32)],
        compiler_params=pltpu.CompilerParams(
            dimension_semantics=("parallel", "arbitrary"), vmem_limit_bytes=VMEM_LIMIT),
        name=name,
    )(lam, subg, q, k, v, km, vm)


def _meta_attn_kernel(lam_ref, subg_ref, q_ref, k_ref, v_ref, o_ref, m_scr, l_scr, acc_scr):
    mask = _causal_mask(N_META, META_PAD, 0)
    for h in range(N_HEADS):
        sl = slice(h * 128, (h + 1) * 128)
        _flash_init(m_scr, l_scr, acc_scr)
        _flash_step(_stack_components(q_ref[:, sl]), k_ref[:, sl], v_ref[:, sl],
                    m_scr, l_scr, acc_scr, mask=mask)
        o_ref[:, sl] = _flash_finish(lam_ref, subg_ref, l_scr, acc_scr, N_META)


def _meta_attn_call(lam, subg, q, k, v, name):
    return pl.pallas_call(
        _meta_attn_kernel,
        out_shape=jax.ShapeDtypeStruct((N_META, ATTN_WIDTH), BF16),
        scratch_shapes=[pltpu.VMEM((2 * N_META, 128), F32), pltpu.VMEM((2 * N_META, 128), F32),
                        pltpu.VMEM((2 * N_META, V_HEAD_DIM), F32)],
        name=name,
    )(lam, subg, q, k, v)


def _conv_kernel(z_ref, halo_ref, prefix_ref, w_ref, b_ref, g_ref, beta_ref, o_ref,
                 sh_scr, *, tm, rows, tiles_per_seq):
    i = pl.program_id(0)
    sh_scr[0, 0:HALO, :] = jnp.where(i % tiles_per_seq == 0, prefix_ref[...], halo_ref[...])
    sh_scr[0, HALO:HALO + tm, :] = z_ref[...]
    span = tm + HALO - SUBLANES
    for r in range(1, SUBLANES):
        sh_scr[r, 0:span, :] = sh_scr[0, r:r + span, :]
    first = HALO - (CONV_WIDTH - 1)
    for c in range(tm // rows):
        acc = jnp.zeros((rows // SUBLANES, SUBLANES, CONV_CH), F32)
        for j in range(CONV_WIDTH):
            a, r = divmod(first + j, SUBLANES)
            tap = sh_scr[r, a * SUBLANES + c * rows:a * SUBLANES + (c + 1) * rows, :]
            acc = acc + w_ref[j] * tap.reshape(rows // SUBLANES, SUBLANES, CONV_CH)
        y = acc.reshape(rows, CONV_CH) + b_ref[...]
        yc = y - jnp.mean(y, axis=-1, keepdims=True)
        var = jnp.mean(yc * yc, axis=-1, keepdims=True)
        y = yc * lax.rsqrt(var + NORM_EPS) * g_ref[...] + beta_ref[...]
        o_ref[c * rows:(c + 1) * rows, :] = (y * jax.nn.sigmoid(y)).astype(BF16)


def _conv_call(z, halo_src, prefix, w, b, g, beta, tm, tiles_per_seq, name):
    t = z.shape[0]
    row = lambda i: (i, 0)
    halo = lambda i: (jnp.maximum(i * (tm // HALO) - 1, 0), 0)
    rows = min(tm, 64)
    return pl.pallas_call(
        functools.partial(_conv_kernel, tm=tm, rows=rows, tiles_per_seq=tiles_per_seq),
        grid=(t // tm,),
        in_specs=[pl.BlockSpec((tm, CONV_CH), row), pl.BlockSpec((HALO, CONV_CH), halo),
                  _resident((HALO, CONV_CH)), _resident((CONV_WIDTH, SUBLANES, CONV_CH)),
                  _resident((1, CONV_CH)), _resident((1, CONV_CH)), _resident((1, CONV_CH))],
        out_specs=pl.BlockSpec((tm, CONV_CH), row),
        out_shape=jax.ShapeDtypeStruct((t, CONV_CH), BF16),
        scratch_shapes=[pltpu.VMEM((SUBLANES, HALO + tm, CONV_CH), F32)],
        compiler_params=pltpu.CompilerParams(
            dimension_semantics=("parallel",), vmem_limit_bytes=VMEM_LIMIT),
        name=name,
    )(z, halo_src, prefix, w, b, g, beta)


def _ffn_kernel(h_ref, a_ref, c_ref, wo_ref, g2_ref, wgu_ref, wd_ref, fg_ref, o_ref,
                hn_scr, act_scr, *, final):
    h1 = (h_ref[...] + _dot(a_ref[...], wo_ref[0:ATTN_WIDTH, :])
          + _dot(c_ref[...], wo_ref[ATTN_WIDTH:ATTN_WIDTH + CONV_CH, :]))
    o_ref[...] = h1
    hn_scr[...] = _rms(h1, g2_ref[...]).astype(BF16)
    for c in range(D_FF // FF_CHUNK):
        sl = slice(c * FF_CHUNK, (c + 1) * FF_CHUNK)
        hn = hn_scr[...]
        gate = _dot(hn, wgu_ref[:, sl])
        up = _dot(hn, wgu_ref[:, D_FF + c * FF_CHUNK:D_FF + (c + 1) * FF_CHUNK])
        act_scr[:, sl] = (gate * jax.nn.sigmoid(gate) * up).astype(BF16)
    h2 = o_ref[...] + _dot(act_scr[...], wd_ref[...])
    if final:
        h2 = _rms(h2, fg_ref[...])
    o_ref[...] = h2


def _ffn_call(h, a, c, wo, g2, wgu, wd, fg, tm, final, name):
    t = h.shape[0]
    row = lambda i: (i, 0)
    return pl.pallas_call(
        functools.partial(_ffn_kernel, final=final),
        grid=(t // tm,),
        in_specs=[pl.BlockSpec((tm, D_MODEL), row), pl.BlockSpec((tm, ATTN_WIDTH), row),
                  pl.BlockSpec((tm, CONV_CH), row), _resident((D_MODEL, D_MODEL)),
                  _resident((1, D_MODEL)), _resident((D_MODEL, 2 * D_FF)),
                  _resident((D_FF, D_MODEL)), _resident((1, D_MODEL))],
        out_specs=pl.BlockSpec((tm, D_MODEL), row),
        out_shape=jax.ShapeDtypeStruct((t, D_MODEL), F32),
        scratch_shapes=[pltpu.VMEM((tm, D_MODEL), BF16), pltpu.VMEM((tm, D_FF), BF16)],
        compiler_params=pltpu.CompilerParams(
            dimension_semantics=("parallel",), vmem_limit_bytes=VMEM_LIMIT),
        name=name,
    )(h, a, c, wo, g2, wgu, wd, fg)


def _rope_tables(length):
    pos = jnp.arange(length, dtype=F32)
    inv = ROPE_THETA ** (-jnp.arange(0, HEAD_DIM, 2, dtype=F32) / HEAD_DIM)
    ang = pos[:, None] * inv[None, :]
    cos = jnp.tile(jnp.cos(ang), (1, 4))
    sin = jnp.tile(jnp.sin(ang), (1, 4))
    first_half = (jnp.arange(128) % HEAD_DIM) < HEAD_DIM // 2
    return cos, jnp.where(first_half, -sin, 0.0), jnp.where(first_half, 0.0, sin)


def kernel(x, meta_tokens, norm1_g, w_in, b_glu, conv_w, conv_b, conv_ln_g, conv_ln_b,
           lam_q1, lam_k1, lam_q2, lam_k2, subln_g, w_out, norm2_g, w_gate_up, w_down,
           final_g):
    bsz, seq, d = x.shape
    depth = w_in.shape[0]
    tm = TOKEN_TILE
    tps = seq // tm
    h = x.reshape(bsz * seq, d)
    hm = meta_tokens.astype(x.dtype)

    tables = _rope_tables(N_META + seq)
    tab_meta = [t[:N_META] for t in tables]
    tab_main = [t[N_META:] for t in tables]

    w_in_b = w_in.astype(BF16)
    w_out_b = w_out.astype(BF16)
    w_gu_b = w_gate_up.astype(BF16)
    w_dn_b = w_down.astype(BF16)
    row = lambda a: a.reshape(1, -1)
    zero_halo = jnp.zeros((HALO, CONV_CH), F32)
    fg = row(final_g)
    pad_keys = lambda a: jnp.pad(a, ((0, META_PAD - N_META), (0, 0)))

    for l in range(depth):
        lam_init = 0.8 - 0.6 * math.exp(-0.3 * l)
        last = l == depth - 1
        lam = jnp.concatenate([
            jnp.stack([lam_q1[l], lam_k1[l], lam_q2[l], lam_k2[l]]).astype(F32),
            jnp.full((1, HEAD_DIM), lam_init, F32), jnp.zeros((3, HEAD_DIM), F32)])
        subg = row(subln_g[l])
        pargs = (row(norm1_g[l]), w_in_b[l], row(b_glu[l]))
        q, k, v, z = _proj_call(h, *pargs, *tab_main, tm, tps, f"proj{l}")
        qm, km, vm, zm = _proj_call(hm, *pargs, *tab_meta, N_META, 1, f"proj_meta{l}")
        km, vm = pad_keys(km), pad_keys(vm)

        attn = _attn_call(lam, subg, q.reshape(bsz, seq, -1), k.reshape(bsz, seq, -1),
                          v.reshape(bsz, seq, -1), km, vm, f"attn{l}")
        attn = attn.reshape(bsz * seq, -1)

        conv_w_rep = jnp.broadcast_to(conv_w[l][:, None, :], (CONV_WIDTH, SUBLANES, CONV_CH))
        cargs = (conv_w_rep, row(conv_b[l]), row(conv_ln_g[l]), row(conv_ln_b[l]))
        prefix = jnp.concatenate([jnp.zeros((HALO - N_META, CONV_CH), F32), zm], axis=0)
        conv = _conv_call(z, z, prefix, *cargs, tm, tps, f"conv{l}")

        fargs = (w_out_b[l], row(norm2_g[l]), w_gu_b[l], w_dn_b[l], fg)
        h = _ffn_call(h, attn, conv, *fargs, tm, last, f"ffn{l}")
        if not last:
            attn_m = _meta_attn_call(lam, subg, qm, km, vm, f"attn_meta{l}")
            conv_m = _conv_call(zm, zero_halo, zero_halo, *cargs, N_META, 1, f"conv_meta{l}")
            hm = _ffn_call(hm, attn_m, conv_m, *fargs, N_META, False, f"ffn_meta{l}")
    return h.reshape(bsz, seq, d)
```

```python
import functools
import math

import jax
import jax.numpy as jnp
from jax import lax
from jax.experimental import pallas as pl
from jax.experimental.pallas import tpu as pltpu

D_MODEL = 1024
N_META = 16
ATTN_WIDTH = 512
CONV_CH = 512
N_HEADS = 4
V_HEAD_DIM = 128
HEAD_DIM = 64
QK_WIDTH = 512
IN_WIDTH = 2 * QK_WIDTH + ATTN_WIDTH + 2 * CONV_CH
CONV_WIDTH = 31
D_FF = 2816
ROPE_THETA = 10000.0
NORM_EPS = 1e-5
LOG2E = math.log2(math.e)
SUBLANES = 8

HALO = 32
META_PAD = 128
TOKEN_TILE = 512
Q_TILE = 512
K_TILE = 512
FF_CHUNK = 256
VMEM_LIMIT = 56 * 1024 * 1024

F32 = jnp.float32
BF16 = jnp.bfloat16


def _dot(a, b):
    return jnp.dot(a, b, preferred_element_type=F32)


def _dot_nt(a, b):
    return lax.dot_general(a, b, (((1,), (1,)), ((), ())), preferred_element_type=F32)


def _rms(x, g):
    ms = jnp.mean(x * x, axis=-1, keepdims=True)
    return x * lax.rsqrt(ms + NORM_EPS) * g


def _resident(shape):
    zeros = (0,) * len(shape)
    return pl.BlockSpec(shape, lambda *_: zeros, pipeline_mode=pl.Buffered(1))


def _proj_kernel(h_ref, g_ref, w_ref, b_ref, cos_ref, sa_ref, sb_ref,
                 q_ref, k_ref, v_ref, z_ref):
    hn = _rms(h_ref[...], g_ref[...]).astype(BF16)
    cos = cos_ref[...]
    sin_lo = sa_ref[...]
    sin_hi = sb_ref[...]

    def rope(t):
        return t * cos + pltpu.roll(t, 96, 1) * sin_lo + pltpu.roll(t, 32, 1) * sin_hi

    qf = _dot(hn, w_ref[:, 0:QK_WIDTH])
    for g in range(N_HEADS):
        sl = slice(g * 128, (g + 1) * 128)
        q_ref[:, sl] = (rope(qf[:, sl]) * (HEAD_DIM ** -0.5 * LOG2E)).astype(BF16)
    kf = _dot(hn, w_ref[:, QK_WIDTH:2 * QK_WIDTH])
    for g in range(N_HEADS):
        sl = slice(g * 128, (g + 1) * 128)
        k_ref[:, sl] = rope(kf[:, sl]).astype(BF16)
    v_ref[...] = _dot(hn, w_ref[:, 2 * QK_WIDTH:2 * QK_WIDTH + ATTN_WIDTH]).astype(BF16)
    u = _dot(hn, w_ref[:, 2 * QK_WIDTH + ATTN_WIDTH:IN_WIDTH]) + b_ref[...]
    z_ref[...] = u[:, :CONV_CH] * jax.nn.sigmoid(u[:, CONV_CH:])


def _proj_call(h, g, w, b, cos, sin_lo, sin_hi, tm, tiles_per_seq, name):
    t = h.shape[0]
    row = lambda i: (i, 0)
    pos = lambda i: (i % tiles_per_seq, 0)
    half = pl.BlockSpec((tm, 512), row)
    tab = pl.BlockSpec((tm, 128), pos)
    return pl.pallas_call(
        _proj_kernel,
        grid=(t // tm,),
        in_specs=[pl.BlockSpec((tm, D_MODEL), row), _resident((1, D_MODEL)),
                  _resident((D_MODEL, IN_WIDTH)), _resident((1, 2 * CONV_CH)),
                  tab, tab, tab],
        out_specs=[half, half, half, half],
        out_shape=[jax.ShapeDtypeStruct((t, 512), BF16)] * 3
        + [jax.ShapeDtypeStruct((t, 512), F32)],
        compiler_params=pltpu.CompilerParams(
            dimension_semantics=("parallel",), vmem_limit_bytes=VMEM_LIMIT),
        name=name,
    )(h, g, w, b, cos, sin_lo, sin_hi)


def _component(q, c):
    lane = lax.broadcasted_iota(jnp.int32, q.shape, 1)
    keep = (lane < HEAD_DIM) if c == 0 else (lane >= HEAD_DIM)
    return jnp.where(keep, q, jnp.zeros_like(q))


def _flash_init(m_ref, acc_ref):
    m_ref[...] = jnp.full(m_ref.shape, -jnp.inf, F32)
    acc_ref[...] = jnp.zeros(acc_ref.shape, F32)


def _flash_step(qc, kj, vj, m_ref, acc_ref, mask=None):
    tk = kj.shape[0]
    s = _dot_nt(qc, kj)
    if mask is not None:
        s = jnp.where(mask, s, -jnp.inf)
    m_prev = m_ref[...]
    m_new = jnp.maximum(m_prev, jnp.max(s, axis=-1, keepdims=True))
    alpha = jnp.exp2(m_prev - m_new)
    p = jnp.exp2(s - jnp.concatenate([m_new] * (tk // 128), axis=1))
    v_ones = jnp.concatenate([vj, jnp.ones((tk, 128), BF16)], axis=1)
    acc_ref[...] = (jnp.concatenate([alpha, alpha], axis=1) * acc_ref[...]
                    + _dot(p.astype(BF16), v_ones))
    m_ref[...] = m_new


def _causal_mask(n, nk, shift):
    r = lax.broadcasted_iota(jnp.int32, (n, nk), 0)
    c = lax.broadcasted_iota(jnp.int32, (n, nk), 1)
    return c - r <= shift


def _flash_finish(lam_ref, subg_ref, acc0_ref, acc1_ref):
    lq1, lk1, lq2, lk2 = (lam_ref[i:i + 1, :] for i in range(4))
    lam_init = lam_ref[4:5, 0:1]
    lam = (jnp.exp(jnp.sum(lq1 * lk1, axis=-1, keepdims=True))
           - jnp.exp(jnp.sum(lq2 * lk2, axis=-1, keepdims=True)) + lam_init)
    o = (acc0_ref[:, 0:128] / acc0_ref[:, 128:256]
         - lam * (acc1_ref[:, 0:128] / acc1_ref[:, 128:256]))
    return (_rms(o, subg_ref[...]) * (1.0 - lam_init)).astype(BF16)


def _attn_kernel(lam_ref, subg_ref, q_ref, k_ref, v_ref, km_ref, vm_ref, o_ref,
                 qc_scr, m_scr, acc_scr, *, tq, tk):
    i = pl.program_id(1)
    heads = [slice(h * 128, (h + 1) * 128) for h in range(N_HEADS)]
    chains = [(h, 2 * h + c) for h in range(N_HEADS) for c in range(2)]
    meta_mask = lax.broadcasted_iota(jnp.int32, (tq, META_PAD), 1) < N_META
    for h, n in chains:
        qc_scr[n] = _component(q_ref[:, heads[h]], n % 2)
        _flash_init(m_scr.at[n], acc_scr.at[n])
        _flash_step(qc_scr[n], km_ref[:, heads[h]], vm_ref[:, heads[h]],
                    m_scr.at[n], acc_scr.at[n], mask=meta_mask)

    def kv_block(j, mask):
        off = pl.multiple_of(j * tk, tk)
        for h, n in chains:
            _flash_step(qc_scr[n], k_ref[pl.ds(off, tk), heads[h]],
                        v_ref[pl.ds(off, tk), heads[h]], m_scr.at[n], acc_scr.at[n], mask=mask)

    n_full = (i * tq) // tk

    def body(j, carry):
        kv_block(2 * j, None)
        kv_block(2 * j + 1, None)
        return carry

    lax.fori_loop(0, n_full // 2, body, 0)

    @pl.when(n_full % 2 == 1)
    def _():
        kv_block(n_full - 1, None)

    kv_block(n_full, _causal_mask(tq, tk, i * tq - n_full * tk))
    for h in range(N_HEADS):
        o_ref[:, heads[h]] = _flash_finish(lam_ref, subg_ref,
                                           acc_scr.at[2 * h], acc_scr.at[2 * h + 1])


def _attn_call(lam, subg, q, k, v, km, vm, name):
    b, s, _ = q.shape
    tq, tk = Q_TILE, K_TILE
    qblk = pl.BlockSpec((None, tq, ATTN_WIDTH), lambda bi, qi: (bi, qi, 0))
    kvblk = pl.BlockSpec((None, s, ATTN_WIDTH), lambda bi, qi: (bi, 0, 0),
                         pipeline_mode=pl.Buffered(1))
    return pl.pallas_call(
        functools.partial(_attn_kernel, tq=tq, tk=tk),
        grid=(b, s // tq),
        in_specs=[_resident((8, HEAD_DIM)), _resident((1, V_HEAD_DIM)),
                  qblk, kvblk, kvblk,
                  _resident((META_PAD, ATTN_WIDTH)), _resident((META_PAD, ATTN_WIDTH))],
        out_specs=qblk,
        out_shape=jax.ShapeDtypeStruct((b, s, ATTN_WIDTH), BF16),
        scratch_shapes=[pltpu.VMEM((2 * N_HEADS, tq, 128), BF16),
                        pltpu.VMEM((2 * N_HEADS, tq, 128), F32),
                        pltpu.VMEM((2 * N_HEADS, tq, 2 * V_HEAD_DIM), F32)],
        compiler_params=pltpu.CompilerParams(
            dimension_semantics=("parallel", "arbitrary"), vmem_limit_bytes=VMEM_LIMIT),
        name=name,
    )(lam, subg, q, k, v, km, vm)


def _meta_attn_kernel(lam_ref, subg_ref, q_ref, k_ref, v_ref, o_ref, m_scr, acc_scr):
    mask = _causal_mask(N_META, META_PAD, 0)
    for h in range(N_HEADS):
        sl = slice(h * 128, (h + 1) * 128)
        for c in range(2):
            _flash_init(m_scr.at[c], acc_scr.at[c])
            _flash_step(_component(q_ref[:, sl], c), k_ref[:, sl], v_ref[:, sl],
                        m_scr.at[c], acc_scr.at[c], mask=mask)
        o_ref[:, sl] = _flash_finish(lam_ref, subg_ref, acc_scr.at[0], acc_scr.at[1])


def _meta_attn_call(lam, subg, q, k, v, name):
    return pl.pallas_call(
        _meta_attn_kernel,
        out_shape=jax.ShapeDtypeStruct((N_META, ATTN_WIDTH), BF16),
        scratch_shapes=[pltpu.VMEM((2, N_META, 128), F32),
                        pltpu.VMEM((2, N_META, 2 * V_HEAD_DIM), F32)],
        name=name,
    )(lam, subg, q, k, v)


def _conv_kernel(z_ref, halo_ref, prefix_ref, w_ref, b_ref, g_ref, beta_ref, o_ref,
                 sh_scr, *, tm, rows, tiles_per_seq):
    i = pl.program_id(0)
    sh_scr[0, 0:HALO, :] = jnp.where(i % tiles_per_seq == 0, prefix_ref[...], halo_ref[...])
    sh_scr[0, HALO:HALO + tm, :] = z_ref[...]
    span = tm + HALO - SUBLANES
    for r in range(1, SUBLANES):
        sh_scr[r, 0:span, :] = sh_scr[0, r:r + span, :]
    first = HALO - (CONV_WIDTH - 1)
    for c in range(tm // rows):
        acc = jnp.zeros((rows // SUBLANES, SUBLANES, CONV_CH), F32)
        for j in range(CONV_WIDTH):
            a, r = divmod(first + j, SUBLANES)
            tap = sh_scr[r, a * SUBLANES + c * rows:a * SUBLANES + (c + 1) * rows, :]
            acc = acc + w_ref[j] * tap.reshape(rows // SUBLANES, SUBLANES, CONV_CH)
        y = acc.reshape(rows, CONV_CH) + b_ref[...]
        yc = y - jnp.mean(y, axis=-1, keepdims=True)
        var = jnp.mean(yc * yc, axis=-1, keepdims=True)
        y = yc * lax.rsqrt(var + NORM_EPS) * g_ref[...] + beta_ref[...]
        o_ref[c * rows:(c + 1) * rows, :] = (y * jax.nn.sigmoid(y)).astype(BF16)


def _conv_call(z, halo_src, prefix, w, b, g, beta, tm, tiles_per_seq, name):
    t = z.shape[0]
    row = lambda i: (i, 0)
    halo = lambda i: (jnp.maximum(i * (tm // HALO) - 1, 0), 0)
    rows = min(tm, 64)
    return pl.pallas_call(
        functools.partial(_conv_kernel, tm=tm, rows=rows, tiles_per_seq=tiles_per_seq),
        grid=(t // tm,),
        in_specs=[pl.BlockSpec((tm, CONV_CH), row), pl.BlockSpec((HALO, CONV_CH), halo),
                  _resident((HALO, CONV_CH)), _resident((CONV_WIDTH, SUBLANES, CONV_CH)),
                  _resident((1, CONV_CH)), _resident((1, CONV_CH)), _resident((1, CONV_CH))],
        out_specs=pl.BlockSpec((tm, CONV_CH), row),
        out_shape=jax.ShapeDtypeStruct((t, CONV_CH), BF16),
        scratch_shapes=[pltpu.VMEM((SUBLANES, HALO + tm, CONV_CH), F32)],
        compiler_params=pltpu.CompilerParams(
            dimension_semantics=("parallel",), vmem_limit_bytes=VMEM_LIMIT),
        name=name,
    )(z, halo_src, prefix, w, b, g, beta)


def _ffn_kernel(h_ref, a_ref, c_ref, wo_ref, g2_ref, wgu_ref, wd_ref, fg_ref, o_ref,
                hn_scr, act_scr, *, final):
    h1 = (h_ref[...] + _dot(a_ref[...], wo_ref[0:ATTN_WIDTH, :])
          + _dot(c_ref[...], wo_ref[ATTN_WIDTH:ATTN_WIDTH + CONV_CH, :]))
    o_ref[...] = h1
    hn_scr[...] = _rms(h1, g2_ref[...]).astype(BF16)
    for c in range(D_FF // FF_CHUNK):
        sl = slice(c * FF_CHUNK, (c + 1) * FF_CHUNK)
        hn = hn_scr[...]
        gate = _dot(hn, wgu_ref[:, sl])
        up = _dot(hn, wgu_ref[:, D_FF + c * FF_CHUNK:D_FF + (c + 1) * FF_CHUNK])
        act_scr[:, sl] = (gate * jax.nn.sigmoid(gate) * up).astype(BF16)
    h2 = o_ref[...] + _dot(act_scr[...], wd_ref[...])
    if final:
        h2 = _rms(h2, fg_ref[...])
    o_ref[...] = h2


def _ffn_call(h, a, c, wo, g2, wgu, wd, fg, tm, final, name):
    t = h.shape[0]
    row = lambda i: (i, 0)
    return pl.pallas_call(
        functools.partial(_ffn_kernel, final=final),
        grid=(t // tm,),
        in_specs=[pl.BlockSpec((tm, D_MODEL), row), pl.BlockSpec((tm, ATTN_WIDTH), row),
                  pl.BlockSpec((tm, CONV_CH), row), _resident((D_MODEL, D_MODEL)),
                  _resident((1, D_MODEL)), _resident((D_MODEL, 2 * D_FF)),
                  _resident((D_FF, D_MODEL)), _resident((1, D_MODEL))],
        out_specs=pl.BlockSpec((tm, D_MODEL), row),
        out_shape=jax.ShapeDtypeStruct((t, D_MODEL), F32),
        scratch_shapes=[pltpu.VMEM((tm, D_MODEL), BF16), pltpu.VMEM((tm, D_FF), BF16)],
        compiler_params=pltpu.CompilerParams(
            dimension_semantics=("parallel",), vmem_limit_bytes=VMEM_LIMIT),
        name=name,
    )(h, a, c, wo, g2, wgu, wd, fg)


def _rope_tables(length):
    pos = jnp.arange(length, dtype=F32)
    inv = ROPE_THETA ** (-jnp.arange(0, HEAD_DIM, 2, dtype=F32) / HEAD_DIM)
    ang = pos[:, None] * inv[None, :]
    cos = jnp.tile(jnp.cos(ang), (1, 4))
    sin = jnp.tile(jnp.sin(ang), (1, 4))
    first_half = (jnp.arange(128) % HEAD_DIM) < HEAD_DIM // 2
    return cos, jnp.where(first_half, -sin, 0.0), jnp.where(first_half, 0.0, sin)


def kernel(x, meta_tokens, norm1_g, w_in, b_glu, conv_w, conv_b, conv_ln_g, conv_ln_b,
           lam_q1, lam_k1, lam_q2, lam_k2, subln_g, w_out, norm2_g, w_gate_up, w_down,
           final_g):
    bsz, seq, d = x.shape
    depth = w_in.shape[0]
    tm = TOKEN_TILE
    tps = seq // tm
    h = x.reshape(bsz * seq, d)
    hm = meta_tokens.astype(x.dtype)

    tables = _rope_tables(N_META + seq)
    tab_meta = [t[:N_META] for t in tables]
    tab_main = [t[N_META:] for t in tables]

    row = lambda a: a.reshape(1, -1)
    zero_halo = jnp.zeros((HALO, CONV_CH), F32)
    fg = row(final_g)
    pad_keys = lambda a: jnp.pad(a, ((0, META_PAD - N_META), (0, 0)))

    for l in range(depth):
        lam_init = 0.8 - 0.6 * math.exp(-0.3 * l)
        last = l == depth - 1
        lam = jnp.concatenate([
            jnp.stack([lam_q1[l], lam_k1[l], lam_q2[l], lam_k2[l]]).astype(F32),
            jnp.full((1, HEAD_DIM), lam_init, F32), jnp.zeros((3, HEAD_DIM), F32)])
        subg = row(subln_g[l])
        pargs = (row(norm1_g[l]), w_in[l].astype(BF16), row(b_glu[l]))
        q, k, v, z = _proj_call(h, *pargs, *tab_main, tm, tps, f"proj{l}")
        qm, km, vm, zm = _proj_call(hm, *pargs, *tab_meta, N_META, 1, f"proj_meta{l}")
        km, vm = pad_keys(km), pad_keys(vm)

        attn = _attn_call(lam, subg, q.reshape(bsz, seq, -1), k.reshape(bsz, seq, -1),
                          v.reshape(bsz, seq, -1), km, vm, f"attn{l}")
        attn = attn.reshape(bsz * seq, -1)

        conv_w_rep = jnp.broadcast_to(conv_w[l][:, None, :], (CONV_WIDTH, SUBLANES, CONV_CH))
        cargs = (conv_w_rep, row(conv_b[l]), row(conv_ln_g[l]), row(conv_ln_b[l]))
        prefix = jnp.concatenate([jnp.zeros((HALO - N_META, CONV_CH), F32), zm], axis=0)
        conv = _conv_call(z, z, prefix, *cargs, tm, tps, f"conv{l}")

        fargs = (w_out[l].astype(BF16), row(norm2_g[l]), w_gate_up[l].astype(BF16),
                 w_down[l].astype(BF16), fg)
        h = _ffn_call(h, attn, conv, *fargs, tm, last, f"ffn{l}")
        if not last:
            attn_m = _meta_attn_call(lam, subg, qm, km, vm, f"attn_meta{l}")
            conv_m = _conv_call(zm, zero_halo, zero_halo, *cargs, N_META, 1, f"conv_meta{l}")
            hm = _ffn_call(hm, attn_m, conv_m, *fargs, N_META, False, f"ffn_meta{l}")
    return h.reshape(bsz, seq, d)
```

```python
import functools
import math

import jax
import jax.numpy as jnp
from jax import lax
from jax.experimental import pallas as pl
from jax.experimental.pallas import tpu as pltpu

D_MODEL = 1024
N_META = 16
ATTN_WIDTH = 512
CONV_CH = 512
N_HEADS = 4
V_HEAD_DIM = 128
HEAD_DIM = 64
QK_WIDTH = 512
IN_WIDTH = 2 * QK_WIDTH + ATTN_WIDTH + 2 * CONV_CH
CONV_WIDTH = 31
D_FF = 2816
ROPE_THETA = 10000.0
NORM_EPS = 1e-5
LOG2E = math.log2(math.e)
SUBLANES = 8

HALO = 32
META_PAD = 128
TOKEN_TILE = 512
Q_TILE = 512
K_TILE = 512
FF_CHUNK = 256
VMEM_LIMIT = 56 * 1024 * 1024

F32 = jnp.float32
BF16 = jnp.bfloat16


def _dot(a, b):
    return jnp.dot(a, b, preferred_element_type=F32)


def _dot_nt(a, b):
    return lax.dot_general(a, b, (((1,), (1,)), ((), ())), preferred_element_type=F32)


def _rms(x, g):
    ms = jnp.mean(x * x, axis=-1, keepdims=True)
    return x * lax.rsqrt(ms + NORM_EPS) * g


def _resident(shape):
    zeros = (0,) * len(shape)
    return pl.BlockSpec(shape, lambda *_: zeros, pipeline_mode=pl.Buffered(1))


def _layer(shape, l):
    index = (l,) + (0,) * len(shape)
    return pl.BlockSpec((None,) + shape, lambda *_: index, pipeline_mode=pl.Buffered(1))


def _proj_kernel(h_ref, g_ref, w_ref, b_ref, cos_ref, sa_ref, sb_ref,
                 q_ref, k_ref, v_ref, z_ref):
    hn = _rms(h_ref[...], g_ref[...]).astype(BF16)
    cos = cos_ref[...]
    sin_lo = sa_ref[...]
    sin_hi = sb_ref[...]

    def rope(t):
        return t * cos + pltpu.roll(t, 96, 1) * sin_lo + pltpu.roll(t, 32, 1) * sin_hi

    qf = _dot(hn, w_ref[:, 0:QK_WIDTH])
    for g in range(N_HEADS):
        sl = slice(g * 128, (g + 1) * 128)
        q_ref[:, sl] = (rope(qf[:, sl]) * (HEAD_DIM ** -0.5 * LOG2E)).astype(BF16)
    kf = _dot(hn, w_ref[:, QK_WIDTH:2 * QK_WIDTH])
    for g in range(N_HEADS):
        sl = slice(g * 128, (g + 1) * 128)
        k_ref[:, sl] = rope(kf[:, sl]).astype(BF16)
    v_ref[...] = _dot(hn, w_ref[:, 2 * QK_WIDTH:2 * QK_WIDTH + ATTN_WIDTH]).astype(BF16)
    u = _dot(hn, w_ref[:, 2 * QK_WIDTH + ATTN_WIDTH:IN_WIDTH]) + b_ref[...]
    z_ref[...] = u[:, :CONV_CH] * jax.nn.sigmoid(u[:, CONV_CH:])


def _proj_call(h, g, w, b, cos, sin_lo, sin_hi, l, tm, tiles_per_seq, name):
    t = h.shape[0]
    row = lambda i: (i, 0)
    pos = lambda i: (i % tiles_per_seq, 0)
    half = pl.BlockSpec((tm, 512), row)
    tab = pl.BlockSpec((tm, 128), pos)
    return pl.pallas_call(
        _proj_kernel,
        grid=(t // tm,),
        in_specs=[pl.BlockSpec((tm, D_MODEL), row), _resident((1, D_MODEL)),
                  _layer((D_MODEL, IN_WIDTH), l), _resident((1, 2 * CONV_CH)),
                  tab, tab, tab],
        out_specs=[half, half, half, half],
        out_shape=[jax.ShapeDtypeStruct((t, 512), BF16)] * 3
        + [jax.ShapeDtypeStruct((t, 512), F32)],
        compiler_params=pltpu.CompilerParams(
            dimension_semantics=("parallel",), vmem_limit_bytes=VMEM_LIMIT),
        name=name,
    )(h, g, w, b, cos, sin_lo, sin_hi)


def _component(q, c):
    lane = lax.broadcasted_iota(jnp.int32, q.shape, 1)
    keep = (lane < HEAD_DIM) if c == 0 else (lane >= HEAD_DIM)
    return jnp.where(keep, q, jnp.zeros_like(q))


def _flash_init(m_ref, acc_ref):
    m_ref[...] = jnp.full(m_ref.shape, -jnp.inf, F32)
    acc_ref[...] = jnp.zeros(acc_ref.shape, F32)


def _flash_step(qc, kj, vj, m_ref, acc_ref, mask=None):
    tk = kj.shape[0]
    s = _dot_nt(qc, kj)
    if mask is not None:
        s = jnp.where(mask, s, -jnp.inf)
    m_prev = m_ref[...]
    m_new = jnp.maximum(m_prev, jnp.max(s, axis=-1, keepdims=True))
    alpha = jnp.exp2(m_prev - m_new)
    p = jnp.exp2(s - jnp.concatenate([m_new] * (tk // 128), axis=1))
    v_ones = jnp.concatenate([vj, jnp.ones((tk, 128), BF16)], axis=1)
    acc_ref[...] = (jnp.concatenate([alpha, alpha], axis=1) * acc_ref[...]
                    + _dot(p.astype(BF16), v_ones))
    m_ref[...] = m_new


def _causal_mask(n, nk, shift):
    r = lax.broadcasted_iota(jnp.int32, (n, nk), 0)
    c = lax.broadcasted_iota(jnp.int32, (n, nk), 1)
    return c - r <= shift


def _flash_finish(lam_ref, subg_ref, acc0_ref, acc1_ref):
    lq1, lk1, lq2, lk2 = (lam_ref[i:i + 1, :] for i in range(4))
    lam_init = lam_ref[4:5, 0:1]
    lam = (jnp.exp(jnp.sum(lq1 * lk1, axis=-1, keepdims=True))
           - jnp.exp(jnp.sum(lq2 * lk2, axis=-1, keepdims=True)) + lam_init)
    o = (acc0_ref[:, 0:128] / acc0_ref[:, 128:256]
         - lam * (acc1_ref[:, 0:128] / acc1_ref[:, 128:256]))
    return (_rms(o, subg_ref[...]) * (1.0 - lam_init)).astype(BF16)


def _attn_kernel(lam_ref, subg_ref, q_ref, k_ref, v_ref, km_ref, vm_ref, o_ref,
                 qc_scr, m_scr, acc_scr, *, tq, tk):
    i = pl.program_id(1)
    heads = [slice(h * 128, (h + 1) * 128) for h in range(N_HEADS)]
    chains = [(h, 2 * h + c) for h in range(N_HEADS) for c in range(2)]
    meta_mask = lax.broadcasted_iota(jnp.int32, (tq, META_PAD), 1) < N_META
    for h, n in chains:
        qc_scr[n] = _component(q_ref[:, heads[h]], n % 2)
        _flash_init(m_scr.at[n], acc_scr.at[n])
        _flash_step(qc_scr[n], km_ref[:, heads[h]], vm_ref[:, heads[h]],
                    m_scr.at[n], acc_scr.at[n], mask=meta_mask)

    def kv_block(j, mask):
        off = pl.multiple_of(j * tk, tk)
        for h, n in chains:
            _flash_step(qc_scr[n], k_ref[pl.ds(off, tk), heads[h]],
                        v_ref[pl.ds(off, tk), heads[h]], m_scr.at[n], acc_scr.at[n], mask=mask)

    n_full = (i * tq) // tk

    def body(j, carry):
        for t in range(4):
            kv_block(4 * j + t, None)
        return carry

    lax.fori_loop(0, n_full // 4, body, 0)
    rest = n_full % 4

    @pl.when(rest >= 2)
    def _():
        kv_block(n_full - rest, None)
        kv_block(n_full - rest + 1, None)

    @pl.when(rest % 2 == 1)
    def _():
        kv_block(n_full - 1, None)

    kv_block(n_full, _causal_mask(tq, tk, i * tq - n_full * tk))
    for h in range(N_HEADS):
        o_ref[:, heads[h]] = _flash_finish(lam_ref, subg_ref,
                                           acc_scr.at[2 * h], acc_scr.at[2 * h + 1])


def _attn_call(lam, subg, q, k, v, km, vm, name):
    b, s, _ = q.shape
    tq, tk = Q_TILE, K_TILE
    qblk = pl.BlockSpec((None, tq, ATTN_WIDTH), lambda bi, qi: (bi, qi, 0))
    kvblk = pl.BlockSpec((None, s, ATTN_WIDTH), lambda bi, qi: (bi, 0, 0),
                         pipeline_mode=pl.Buffered(1))
    return pl.pallas_call(
        functools.partial(_attn_kernel, tq=tq, tk=tk),
        grid=(b, s // tq),
        in_specs=[_resident((8, HEAD_DIM)), _resident((1, V_HEAD_DIM)),
                  qblk, kvblk, kvblk,
                  _resident((META_PAD, ATTN_WIDTH)), _resident((META_PAD, ATTN_WIDTH))],
        out_specs=qblk,
        out_shape=jax.ShapeDtypeStruct((b, s, ATTN_WIDTH), BF16),
        scratch_shapes=[pltpu.VMEM((2 * N_HEADS, tq, 128), BF16),
                        pltpu.VMEM((2 * N_HEADS, tq, 128), F32),
                        pltpu.VMEM((2 * N_HEADS, tq, 2 * V_HEAD_DIM), F32)],
        compiler_params=pltpu.CompilerParams(
            dimension_semantics=("parallel", "arbitrary"), vmem_limit_bytes=VMEM_LIMIT),
        name=name,
    )(lam, subg, q, k, v, km, vm)


def _meta_attn_kernel(lam_ref, subg_ref, q_ref, k_ref, v_ref, o_ref, m_scr, acc_scr):
    mask = _causal_mask(N_META, META_PAD, 0)
    for h in range(N_HEADS):
        sl = slice(h * 128, (h + 1) * 128)
        for c in range(2):
            _flash_init(m_scr.at[c], acc_scr.at[c])
            _flash_step(_component(q_ref[:, sl], c), k_ref[:, sl], v_ref[:, sl],
                        m_scr.at[c], acc_scr.at[c], mask=mask)
        o_ref[:, sl] = _flash_finish(lam_ref, subg_ref, acc_scr.at[0], acc_scr.at[1])


def _meta_attn_call(lam, subg, q, k, v, name):
    return pl.pallas_call(
        _meta_attn_kernel,
        out_shape=jax.ShapeDtypeStruct((N_META, ATTN_WIDTH), BF16),
        scratch_shapes=[pltpu.VMEM((2, N_META, 128), F32),
                        pltpu.VMEM((2, N_META, 2 * V_HEAD_DIM), F32)],
        name=name,
    )(lam, subg, q, k, v)


def _conv_kernel(z_ref, halo_ref, prefix_ref, w_ref, b_ref, g_ref, beta_ref, o_ref,
                 sh_scr, *, tm, rows, tiles_per_seq):
    i = pl.program_id(0)
    sh_scr[0, 0:HALO, :] = jnp.where(i % tiles_per_seq == 0, prefix_ref[...], halo_ref[...])
    sh_scr[0, HALO:HALO + tm, :] = z_ref[...]
    span = tm + HALO - SUBLANES
    for r in range(1, SUBLANES):
        sh_scr[r, 0:span, :] = sh_scr[0, r:r + span, :]
    first = HALO - (CONV_WIDTH - 1)
    for c in range(tm // rows):
        acc = jnp.zeros((rows // SUBLANES, SUBLANES, CONV_CH), F32)
        for j in range(CONV_WIDTH):
            a, r = divmod(first + j, SUBLANES)
            tap = sh_scr[r, a * SUBLANES + c * rows:a * SUBLANES + (c + 1) * rows, :]
            acc = acc + w_ref[j] * tap.reshape(rows // SUBLANES, SUBLANES, CONV_CH)
        y = acc.reshape(rows, CONV_CH) + b_ref[...]
        yc = y - jnp.mean(y, axis=-1, keepdims=True)
        var = jnp.mean(yc * yc, axis=-1, keepdims=True)
        y = yc * lax.rsqrt(var + NORM_EPS) * g_ref[...] + beta_ref[...]
        o_ref[c * rows:(c + 1) * rows, :] = (y * jax.nn.sigmoid(y)).astype(BF16)


def _conv_call(z, halo_src, prefix, w, b, g, beta, tm, tiles_per_seq, name):
    t = z.shape[0]
    row = lambda i: (i, 0)
    halo = lambda i: (jnp.maximum(i * (tm // HALO) - 1, 0), 0)
    rows = min(tm, 64)
    return pl.pallas_call(
        functools.partial(_conv_kernel, tm=tm, rows=rows, tiles_per_seq=tiles_per_seq),
        grid=(t // tm,),
        in_specs=[pl.BlockSpec((tm, CONV_CH), row), pl.BlockSpec((HALO, CONV_CH), halo),
                  _resident((HALO, CONV_CH)), _resident((CONV_WIDTH, SUBLANES, CONV_CH)),
                  _resident((1, CONV_CH)), _resident((1, CONV_CH)), _resident((1, CONV_CH))],
        out_specs=pl.BlockSpec((tm, CONV_CH), row),
        out_shape=jax.ShapeDtypeStruct((t, CONV_CH), BF16),
        scratch_shapes=[pltpu.VMEM((SUBLANES, HALO + tm, CONV_CH), F32)],
        compiler_params=pltpu.CompilerParams(
            dimension_semantics=("parallel",), vmem_limit_bytes=VMEM_LIMIT),
        name=name,
    )(z, halo_src, prefix, w, b, g, beta)


def _ffn_kernel(h_ref, a_ref, c_ref, wo_ref, g2_ref, wgu_ref, wd_ref, fg_ref, o_ref,
                hn_scr, act_scr, *, final):
    h1 = (h_ref[...] + _dot(a_ref[...], wo_ref[0:ATTN_WIDTH, :])
          + _dot(c_ref[...], wo_ref[ATTN_WIDTH:ATTN_WIDTH + CONV_CH, :]))
    o_ref[...] = h1
    hn_scr[...] = _rms(h1, g2_ref[...]).astype(BF16)
    for c in range(D_FF // FF_CHUNK):
        sl = slice(c * FF_CHUNK, (c + 1) * FF_CHUNK)
        hn = hn_scr[...]
        gate = _dot(hn, wgu_ref[:, sl])
        up = _dot(hn, wgu_ref[:, D_FF + c * FF_CHUNK:D_FF + (c + 1) * FF_CHUNK])
        act_scr[:, sl] = (gate * jax.nn.sigmoid(gate) * up).astype(BF16)
    h2 = o_ref[...] + _dot(act_scr[...], wd_ref[...])
    if final:
        h2 = _rms(h2, fg_ref[...])
    o_ref[...] = h2


def _ffn_call(h, a, c, wo, g2, wgu, wd, fg, l, tm, final, name):
    t = h.shape[0]
    row = lambda i: (i, 0)
    return pl.pallas_call(
        functools.partial(_ffn_kernel, final=final),
        grid=(t // tm,),
        in_specs=[pl.BlockSpec((tm, D_MODEL), row), pl.BlockSpec((tm, ATTN_WIDTH), row),
                  pl.BlockSpec((tm, CONV_CH), row), _layer((D_MODEL, D_MODEL), l),
                  _resident((1, D_MODEL)), _layer((D_MODEL, 2 * D_FF), l),
                  _layer((D_FF, D_MODEL), l), _resident((1, D_MODEL))],
        out_specs=pl.BlockSpec((tm, D_MODEL), row),
        out_shape=jax.ShapeDtypeStruct((t, D_MODEL), F32),
        scratch_shapes=[pltpu.VMEM((tm, D_MODEL), BF16), pltpu.VMEM((tm, D_FF), BF16)],
        compiler_params=pltpu.CompilerParams(
            dimension_semantics=("parallel",), vmem_limit_bytes=VMEM_LIMIT),
        name=name,
    )(h, a, c, wo, g2, wgu, wd, fg)


def _rope_tables(length):
    pos = jnp.arange(length, dtype=F32)
    inv = ROPE_THETA ** (-jnp.arange(0, HEAD_DIM, 2, dtype=F32) / HEAD_DIM)
    ang = pos[:, None] * inv[None, :]
    cos = jnp.tile(jnp.cos(ang), (1, 4))
    sin = jnp.tile(jnp.sin(ang), (1, 4))
    first_half = (jnp.arange(128) % HEAD_DIM) < HEAD_DIM // 2
    return cos, jnp.where(first_half, -sin, 0.0), jnp.where(first_half, 0.0, sin)


def kernel(x, meta_tokens, norm1_g, w_in, b_glu, conv_w, conv_b, conv_ln_g, conv_ln_b,
           lam_q1, lam_k1, lam_q2, lam_k2, subln_g, w_out, norm2_g, w_gate_up, w_down,
           final_g):
    bsz, seq, d = x.shape
    depth = w_in.shape[0]
    tm = TOKEN_TILE
    tps = seq // tm
    h = x.reshape(bsz * seq, d)
    hm = meta_tokens.astype(x.dtype)

    tables = _rope_tables(N_META + seq)
    tab_meta = [t[:N_META] for t in tables]
    tab_main = [t[N_META:] for t in tables]

    w_in_b = w_in.astype(BF16)
    w_out_b = w_out.astype(BF16)
    w_gu_b = w_gate_up.astype(BF16)
    w_dn_b = w_down.astype(BF16)
    row = lambda a: a.reshape(1, -1)
    zero_halo = jnp.zeros((HALO, CONV_CH), F32)
    fg = row(final_g)
    pad_keys = lambda a: jnp.pad(a, ((0, META_PAD - N_META), (0, 0)))

    for l in range(depth):
        lam_init = 0.8 - 0.6 * math.exp(-0.3 * l)
        last = l == depth - 1
        lam = jnp.concatenate([
            jnp.stack([lam_q1[l], lam_k1[l], lam_q2[l], lam_k2[l]]).astype(F32),
            jnp.full((1, HEAD_DIM), lam_init, F32), jnp.zeros((3, HEAD_DIM), F32)])
        subg = row(subln_g[l])
        pargs = (row(norm1_g[l]), w_in_b, row(b_glu[l]))
        q, k, v, z = _proj_call(h, *pargs, *tab_main, l, tm, tps, f"proj{l}")
        qm, km, vm, zm = _proj_call(hm, *pargs, *tab_meta, l, N_META, 1, f"proj_meta{l}")
        km, vm = pad_keys(km), pad_keys(vm)

        attn = _attn_call(lam, subg, q.reshape(bsz, seq, -1), k.reshape(bsz, seq, -1),
                          v.reshape(bsz, seq, -1), km, vm, f"attn{l}")
        attn = attn.reshape(bsz * seq, -1)

        conv_w_rep = jnp.broadcast_to(conv_w[l][:, None, :], (CONV_WIDTH, SUBLANES, CONV_CH))
        cargs = (conv_w_rep, row(conv_b[l]), row(conv_ln_g[l]), row(conv_ln_b[l]))
        prefix = jnp.concatenate([jnp.zeros((HALO - N_META, CONV_CH), F32), zm], axis=0)
        conv = _conv_call(z, z, prefix, *cargs, tm, tps, f"conv{l}")

        fargs = (w_out_b, row(norm2_g[l]), w_gu_b, w_dn_b, fg, l)
        h = _ffn_call(h, attn, conv, *fargs, tm, last, f"ffn{l}")
        if not last:
            attn_m = _meta_attn_call(lam, subg, qm, km, vm, f"attn_meta{l}")
            conv_m = _conv_call(zm, zero_halo, zero_halo, *cargs, N_META, 1, f"conv_meta{l}")
            hm = _ffn_call(hm, attn_m, conv_m, *fargs, N_META, False, f"ffn_meta{l}")
    return h.reshape(bsz, seq, d)
```

```python
import functools
import math

import jax
import jax.numpy as jnp
from jax import lax
from jax.experimental import pallas as pl
from jax.experimental.pallas import tpu as pltpu

D_MODEL = 1024
N_META = 16
ATTN_WIDTH = 512
CONV_CH = 512
N_HEADS = 4
V_HEAD_DIM = 128
HEAD_DIM = 64
QK_WIDTH = 512
IN_WIDTH = 2 * QK_WIDTH + ATTN_WIDTH + 2 * CONV_CH
CONV_WIDTH = 31
D_FF = 2816
ROPE_THETA = 10000.0
NORM_EPS = 1e-5
LOG2E = math.log2(math.e)
SUBLANES = 8

HALO = 32
META_PAD = 128
TOKEN_TILE = 512
PROJ_TILE = 1024
Q_TILE = 512
K_TILE = 512
FF_CHUNK = 256
VMEM_LIMIT = 56 * 1024 * 1024

F32 = jnp.float32
BF16 = jnp.bfloat16


def _dot(a, b):
    return jnp.dot(a, b, preferred_element_type=F32)


def _dot_nt(a, b):
    return lax.dot_general(a, b, (((1,), (1,)), ((), ())), preferred_element_type=F32)


def _rms(x, g):
    ms = jnp.mean(x * x, axis=-1, keepdims=True)
    return x * lax.rsqrt(ms + NORM_EPS) * g


def _resident(shape):
    zeros = (0,) * len(shape)
    return pl.BlockSpec(shape, lambda *_: zeros, pipeline_mode=pl.Buffered(1))


def _layer(shape, l):
    index = (l,) + (0,) * len(shape)
    return pl.BlockSpec((None,) + shape, lambda *_: index, pipeline_mode=pl.Buffered(1))


def _proj_kernel(h_ref, g_ref, w_ref, b_ref, cos_ref, sa_ref, sb_ref,
                 q_ref, k_ref, v_ref, z_ref):
    hn = _rms(h_ref[...], g_ref[...]).astype(BF16)
    cos = cos_ref[...]
    sin_lo = sa_ref[...]
    sin_hi = sb_ref[...]

    def rope(t):
        return t * cos + pltpu.roll(t, 96, 1) * sin_lo + pltpu.roll(t, 32, 1) * sin_hi

    u = _dot(hn, w_ref[:, 2 * QK_WIDTH + ATTN_WIDTH:IN_WIDTH]) + b_ref[...]
    z_ref[...] = u[:, :CONV_CH] * jax.nn.sigmoid(u[:, CONV_CH:])
    qf = _dot(hn, w_ref[:, 0:QK_WIDTH])
    for g in range(N_HEADS):
        sl = slice(g * 128, (g + 1) * 128)
        q_ref[:, sl] = (rope(qf[:, sl]) * (HEAD_DIM ** -0.5 * LOG2E)).astype(BF16)
    kf = _dot(hn, w_ref[:, QK_WIDTH:2 * QK_WIDTH])
    for g in range(N_HEADS):
        sl = slice(g * 128, (g + 1) * 128)
        k_ref[:, sl] = rope(kf[:, sl]).astype(BF16)
    v_ref[...] = _dot(hn, w_ref[:, 2 * QK_WIDTH:2 * QK_WIDTH + ATTN_WIDTH]).astype(BF16)


def _proj_call(h, g, w, b, cos, sin_lo, sin_hi, l, tm, tiles_per_seq, name):
    t = h.shape[0]
    row = lambda i: (i, 0)
    pos = lambda i: (i % tiles_per_seq, 0)
    half = pl.BlockSpec((tm, 512), row)
    tab = pl.BlockSpec((tm, 128), pos)
    return pl.pallas_call(
        _proj_kernel,
        grid=(t // tm,),
        in_specs=[pl.BlockSpec((tm, D_MODEL), row), _resident((1, D_MODEL)),
                  _layer((D_MODEL, IN_WIDTH), l), _resident((1, 2 * CONV_CH)),
                  tab, tab, tab],
        out_specs=[half, half, half, half],
        out_shape=[jax.ShapeDtypeStruct((t, 512), BF16)] * 3
        + [jax.ShapeDtypeStruct((t, 512), F32)],
        compiler_params=pltpu.CompilerParams(
            dimension_semantics=("parallel",), vmem_limit_bytes=VMEM_LIMIT),
        name=name,
    )(h, g, w, b, cos, sin_lo, sin_hi)


def _component(q, c):
    lane = lax.broadcasted_iota(jnp.int32, q.shape, 1)
    keep = (lane < HEAD_DIM) if c == 0 else (lane >= HEAD_DIM)
    return jnp.where(keep, q, jnp.zeros_like(q))


def _flash_init(m_ref, acc_ref):
    m_ref[...] = jnp.full(m_ref.shape, -jnp.inf, F32)
    acc_ref[...] = jnp.zeros(acc_ref.shape, F32)


def _flash_step(qc, kj, vj, m_ref, acc_ref, mask=None):
    tk = kj.shape[0]
    s = _dot_nt(qc, kj)
    if mask is not None:
        s = jnp.where(mask, s, -jnp.inf)
    m_prev = m_ref[...]
    m_new = jnp.maximum(m_prev, jnp.max(s, axis=-1, keepdims=True))
    alpha = jnp.exp2(m_prev - m_new)
    p = jnp.exp2(s - jnp.concatenate([m_new] * (tk // 128), axis=1))
    v_ones = jnp.concatenate([vj, jnp.ones((tk, 128), BF16)], axis=1)
    acc_ref[...] = (jnp.concatenate([alpha, alpha], axis=1) * acc_ref[...]
                    + _dot(p.astype(BF16), v_ones))
    m_ref[...] = m_new


def _causal_mask(n, nk, shift):
    r = lax.broadcasted_iota(jnp.int32, (n, nk), 0)
    c = lax.broadcasted_iota(jnp.int32, (n, nk), 1)
    return c - r <= shift


def _flash_finish(lam_ref, subg_ref, acc0_ref, acc1_ref):
    lq1, lk1, lq2, lk2 = (lam_ref[i:i + 1, :] for i in range(4))
    lam_init = lam_ref[4:5, 0:1]
    lam = (jnp.exp(jnp.sum(lq1 * lk1, axis=-1, keepdims=True))
           - jnp.exp(jnp.sum(lq2 * lk2, axis=-1, keepdims=True)) + lam_init)
    o = (acc0_ref[:, 0:128] / acc0_ref[:, 128:256]
         - lam * (acc1_ref[:, 0:128] / acc1_ref[:, 128:256]))
    return (_rms(o, subg_ref[...]) * (1.0 - lam_init)).astype(BF16)


def _attn_kernel(lam_ref, subg_ref, q_ref, k_ref, v_ref, km_ref, vm_ref, o_ref,
                 qc_scr, m_scr, acc_scr, *, tq, tk):
    i = pl.program_id(1)
    heads = [slice(h * 128, (h + 1) * 128) for h in range(N_HEADS)]
    chains = [(h, 2 * h + c) for h in range(N_HEADS) for c in range(2)]
    meta_mask = lax.broadcasted_iota(jnp.int32, (tq, META_PAD), 1) < N_META
    for h, n in chains:
        qc_scr[n] = _component(q_ref[:, heads[h]], n % 2)
        _flash_init(m_scr.at[n], acc_scr.at[n])
        _flash_step(qc_scr[n], km_ref[:, heads[h]], vm_ref[:, heads[h]],
                    m_scr.at[n], acc_scr.at[n], mask=meta_mask)

    def kv_block(j, mask):
        off = pl.multiple_of(j * tk, tk)
        for h, n in chains:
            _flash_step(qc_scr[n], k_ref[pl.ds(off, tk), heads[h]],
                        v_ref[pl.ds(off, tk), heads[h]], m_scr.at[n], acc_scr.at[n], mask=mask)

    n_full = (i * tq) // tk

    def body(j, carry):
        for t in range(8):
            kv_block(8 * j + t, None)
        return carry

    lax.fori_loop(0, n_full // 8, body, 0)
    done = n_full - n_full % 8
    for run in (4, 2, 1):
        start = done

        @pl.when((n_full // run) % 2 == 1)
        def _():
            for t in range(run):
                kv_block(start + t, None)

        done = done + jnp.where((n_full // run) % 2 == 1, run, 0)

    kv_block(n_full, _causal_mask(tq, tk, i * tq - n_full * tk))
    for h in range(N_HEADS):
        o_ref[:, heads[h]] = _flash_finish(lam_ref, subg_ref,
                                           acc_scr.at[2 * h], acc_scr.at[2 * h + 1])


def _attn_call(lam, subg, q, k, v, km, vm, name):
    b, s, _ = q.shape
    tq, tk = Q_TILE, K_TILE
    qblk = pl.BlockSpec((None, tq, ATTN_WIDTH), lambda bi, qi: (bi, qi, 0))
    kvblk = pl.BlockSpec((None, s, ATTN_WIDTH), lambda bi, qi: (bi, 0, 0),
                         pipeline_mode=pl.Buffered(1))
    return pl.pallas_call(
        functools.partial(_attn_kernel, tq=tq, tk=tk),
        grid=(b, s // tq),
        in_specs=[_resident((8, HEAD_DIM)), _resident((1, V_HEAD_DIM)),
                  qblk, kvblk, kvblk,
                  _resident((META_PAD, ATTN_WIDTH)), _resident((META_PAD, ATTN_WIDTH))],
        out_specs=qblk,
        out_shape=jax.ShapeDtypeStruct((b, s, ATTN_WIDTH), BF16),
        scratch_shapes=[pltpu.VMEM((2 * N_HEADS, tq, 128), BF16),
                        pltpu.VMEM((2 * N_HEADS, tq, 128), F32),
                        pltpu.VMEM((2 * N_HEADS, tq, 2 * V_HEAD_DIM), F32)],
        compiler_params=pltpu.CompilerParams(
            dimension_semantics=("parallel", "arbitrary"), vmem_limit_bytes=VMEM_LIMIT),
        name=name,
    )(lam, subg, q, k, v, km, vm)


def _meta_attn_kernel(lam_ref, subg_ref, q_ref, k_ref, v_ref, o_ref, m_scr, acc_scr):
    mask = _causal_mask(N_META, META_PAD, 0)
    for h in range(N_HEADS):
        sl = slice(h * 128, (h + 1) * 128)
        for c in range(2):
            _flash_init(m_scr.at[c], acc_scr.at[c])
            _flash_step(_component(q_ref[:, sl], c), k_ref[:, sl], v_ref[:, sl],
                        m_scr.at[c], acc_scr.at[c], mask=mask)
        o_ref[:, sl] = _flash_finish(lam_ref, subg_ref, acc_scr.at[0], acc_scr.at[1])


def _meta_attn_call(lam, subg, q, k, v, name):
    return pl.pallas_call(
        _meta_attn_kernel,
        out_shape=jax.ShapeDtypeStruct((N_META, ATTN_WIDTH), BF16),
        scratch_shapes=[pltpu.VMEM((2, N_META, 128), F32),
                        pltpu.VMEM((2, N_META, 2 * V_HEAD_DIM), F32)],
        name=name,
    )(lam, subg, q, k, v)


def _conv_kernel(z_ref, halo_ref, prefix_ref, w_ref, b_ref, g_ref, beta_ref, o_ref,
                 sh_scr, *, tm, rows, tiles_per_seq):
    i = pl.program_id(0)
    sh_scr[0, 0:HALO, :] = jnp.where(i % tiles_per_seq == 0, prefix_ref[...], halo_ref[...])
    sh_scr[0, HALO:HALO + tm, :] = z_ref[...]
    span = tm + HALO - SUBLANES
    for r in range(1, SUBLANES):
        sh_scr[r, 0:span, :] = sh_scr[0, r:r + span, :]
    first = HALO - (CONV_WIDTH - 1)
    for c in range(tm // rows):
        acc = jnp.zeros((rows // SUBLANES, SUBLANES, CONV_CH), F32)
        for j in range(CONV_WIDTH):
            a, r = divmod(first + j, SUBLANES)
            tap = sh_scr[r, a * SUBLANES + c * rows:a * SUBLANES + (c + 1) * rows, :]
            acc = acc + w_ref[j] * tap.reshape(rows // SUBLANES, SUBLANES, CONV_CH)
        y = acc.reshape(rows, CONV_CH) + b_ref[...]
        yc = y - jnp.mean(y, axis=-1, keepdims=True)
        var = jnp.mean(yc * yc, axis=-1, keepdims=True)
        y = yc * lax.rsqrt(var + NORM_EPS) * g_ref[...] + beta_ref[...]
        o_ref[c * rows:(c + 1) * rows, :] = (y * jax.nn.sigmoid(y)).astype(BF16)


def _conv_call(z, halo_src, prefix, w, b, g, beta, tm, tiles_per_seq, name):
    t = z.shape[0]
    row = lambda i: (i, 0)
    halo = lambda i: (jnp.maximum(i * (tm // HALO) - 1, 0), 0)
    rows = min(tm, 64)
    return pl.pallas_call(
        functools.partial(_conv_kernel, tm=tm, rows=rows, tiles_per_seq=tiles_per_seq),
        grid=(t // tm,),
        in_specs=[pl.BlockSpec((tm, CONV_CH), row), pl.BlockSpec((HALO, CONV_CH), halo),
                  _resident((HALO, CONV_CH)), _resident((CONV_WIDTH, SUBLANES, CONV_CH)),
                  _resident((1, CONV_CH)), _resident((1, CONV_CH)), _resident((1, CONV_CH))],
        out_specs=pl.BlockSpec((tm, CONV_CH), row),
        out_shape=jax.ShapeDtypeStruct((t, CONV_CH), BF16),
        scratch_shapes=[pltpu.VMEM((SUBLANES, HALO + tm, CONV_CH), F32)],
        compiler_params=pltpu.CompilerParams(
            dimension_semantics=("parallel",), vmem_limit_bytes=VMEM_LIMIT),
        name=name,
    )(z, halo_src, prefix, w, b, g, beta)


def _ffn_kernel(h_ref, a_ref, c_ref, wo_ref, g2_ref, wgu_ref, wd_ref, fg_ref, o_ref,
                hn_scr, act_scr, *, final):
    h1 = (h_ref[...] + _dot(a_ref[...], wo_ref[0:ATTN_WIDTH, :])
          + _dot(c_ref[...], wo_ref[ATTN_WIDTH:ATTN_WIDTH + CONV_CH, :]))
    o_ref[...] = h1
    hn_scr[...] = _rms(h1, g2_ref[...]).astype(BF16)
    for c in range(D_FF // FF_CHUNK):
        sl = slice(c * FF_CHUNK, (c + 1) * FF_CHUNK)
        hn = hn_scr[...]
        gate = _dot(hn, wgu_ref[:, sl])
        up = _dot(hn, wgu_ref[:, D_FF + c * FF_CHUNK:D_FF + (c + 1) * FF_CHUNK])
        act_scr[:, sl] = (gate * jax.nn.sigmoid(gate) * up).astype(BF16)
    h2 = o_ref[...] + _dot(act_scr[...], wd_ref[...])
    if final:
        h2 = _rms(h2, fg_ref[...])
    o_ref[...] = h2


def _ffn_call(h, a, c, wo, g2, wgu, wd, fg, l, tm, final, name):
    t = h.shape[0]
    row = lambda i: (i, 0)
    return pl.pallas_call(
        functools.partial(_ffn_kernel, final=final),
        grid=(t // tm,),
        in_specs=[pl.BlockSpec((tm, D_MODEL), row), pl.BlockSpec((tm, ATTN_WIDTH), row),
                  pl.BlockSpec((tm, CONV_CH), row), _layer((D_MODEL, D_MODEL), l),
                  _resident((1, D_MODEL)), _layer((D_MODEL, 2 * D_FF), l),
                  _layer((D_FF, D_MODEL), l), _resident((1, D_MODEL))],
        out_specs=pl.BlockSpec((tm, D_MODEL), row),
        out_shape=jax.ShapeDtypeStruct((t, D_MODEL), F32),
        scratch_shapes=[pltpu.VMEM((tm, D_MODEL), BF16), pltpu.VMEM((tm, D_FF), BF16)],
        compiler_params=pltpu.CompilerParams(
            dimension_semantics=("parallel",), vmem_limit_bytes=VMEM_LIMIT),
        name=name,
    )(h, a, c, wo, g2, wgu, wd, fg)


def _rope_tables(length):
    pos = jnp.arange(length, dtype=F32)
    inv = ROPE_THETA ** (-jnp.arange(0, HEAD_DIM, 2, dtype=F32) / HEAD_DIM)
    ang = pos[:, None] * inv[None, :]
    cos = jnp.tile(jnp.cos(ang), (1, 4))
    sin = jnp.tile(jnp.sin(ang), (1, 4))
    first_half = (jnp.arange(128) % HEAD_DIM) < HEAD_DIM // 2
    return cos, jnp.where(first_half, -sin, 0.0), jnp.where(first_half, 0.0, sin)


def kernel(x, meta_tokens, norm1_g, w_in, b_glu, conv_w, conv_b, conv_ln_g, conv_ln_b,
           lam_q1, lam_k1, lam_q2, lam_k2, subln_g, w_out, norm2_g, w_gate_up, w_down,
           final_g):
    bsz, seq, d = x.shape
    depth = w_in.shape[0]
    tm = TOKEN_TILE
    tps = seq // tm
    h = x.reshape(bsz * seq, d)
    hm = meta_tokens.astype(x.dtype)

    tables = _rope_tables(N_META + seq)
    tab_meta = [t[:N_META] for t in tables]
    tab_main = [t[N_META:] for t in tables]

    w_in_b = w_in.astype(BF16)
    w_out_b = w_out.astype(BF16)
    w_gu_b = w_gate_up.astype(BF16)
    w_dn_b = w_down.astype(BF16)
    row = lambda a: a.reshape(1, -1)
    zero_halo = jnp.zeros((HALO, CONV_CH), F32)
    fg = row(final_g)
    pad_keys = lambda a: jnp.pad(a, ((0, META_PAD - N_META), (0, 0)))

    for l in range(depth):
        lam_init = 0.8 - 0.6 * math.exp(-0.3 * l)
        last = l == depth - 1
        lam = jnp.concatenate([
            jnp.stack([lam_q1[l], lam_k1[l], lam_q2[l], lam_k2[l]]).astype(F32),
            jnp.full((1, HEAD_DIM), lam_init, F32), jnp.zeros((3, HEAD_DIM), F32)])
        subg = row(subln_g[l])
        pargs = (row(norm1_g[l]), w_in_b, row(b_glu[l]))
        q, k, v, z = _proj_call(h, *pargs, *tab_main, l, PROJ_TILE, seq // PROJ_TILE,
                                f"proj{l}")
        qm, km, vm, zm = _proj_call(hm, *pargs, *tab_meta, l, N_META, 1, f"proj_meta{l}")
        km, vm = pad_keys(km), pad_keys(vm)

        attn = _attn_call(lam, subg, q.reshape(bsz, seq, -1), k.reshape(bsz, seq, -1),
                          v.reshape(bsz, seq, -1), km, vm, f"attn{l}")
        attn = attn.reshape(bsz * seq, -1)

        conv_w_rep = jnp.broadcast_to(conv_w[l][:, None, :], (CONV_WIDTH, SUBLANES, CONV_CH))
        cargs = (conv_w_rep, row(conv_b[l]), row(conv_ln_g[l]), row(conv_ln_b[l]))
        prefix = jnp.concatenate([jnp.zeros((HALO - N_META, CONV_CH), F32), zm], axis=0)
        conv = _conv_call(z, z, prefix, *cargs, tm, tps, f"conv{l}")

        fargs = (w_out_b, row(norm2_g[l]), w_gu_b, w_dn_b, fg, l)
        h = _ffn_call(h, attn, conv, *fargs, tm, last, f"ffn{l}")
        if not last:
            attn_m = _meta_attn_call(lam, subg, qm, km, vm, f"attn_meta{l}")
            conv_m = _conv_call(zm, zero_halo, zero_halo, *cargs, N_META, 1, f"conv_meta{l}")
            hm = _ffn_call(hm, attn_m, conv_m, *fargs, N_META, False, f"ffn_meta{l}")
    return h.reshape(bsz, seq, d)
```

```python
import functools
import math

import jax
import jax.numpy as jnp
from jax import lax
from jax.experimental import pallas as pl
from jax.experimental.pallas import tpu as pltpu

D_MODEL = 1024
N_META = 16
ATTN_WIDTH = 512
CONV_CH = 512
N_HEADS = 4
V_HEAD_DIM = 128
HEAD_DIM = 64
QK_WIDTH = 512
IN_WIDTH = 2 * QK_WIDTH + ATTN_WIDTH + 2 * CONV_CH
CONV_WIDTH = 31
D_FF = 2816
ROPE_THETA = 10000.0
NORM_EPS = 1e-5
LOG2E = math.log2(math.e)
SUBLANES = 8

HALO = 32
META_PAD = 128
TOKEN_TILE = 512
PROJ_TILE = 1024
Q_TILE = 512
K_TILE = 512
FF_CHUNK = 256
VMEM_LIMIT = 56 * 1024 * 1024

F32 = jnp.float32
BF16 = jnp.bfloat16


def _dot(a, b):
    return jnp.dot(a, b, preferred_element_type=F32)


def _dot_nt(a, b):
    return lax.dot_general(a, b, (((1,), (1,)), ((), ())), preferred_element_type=F32)


def _rms(x, g):
    ms = jnp.mean(x * x, axis=-1, keepdims=True)
    return x * lax.rsqrt(ms + NORM_EPS) * g


def _resident(shape):
    zeros = (0,) * len(shape)
    return pl.BlockSpec(shape, lambda *_: zeros, pipeline_mode=pl.Buffered(1))


def _layer(shape, l):
    index = (l,) + (0,) * len(shape)
    return pl.BlockSpec((None,) + shape, lambda *_: index, pipeline_mode=pl.Buffered(1))


def _proj_kernel(h_ref, g_ref, w_ref, b_ref, cos_ref, sa_ref, sb_ref,
                 q_ref, k_ref, v_ref, z_ref):
    hn = _rms(h_ref[...], g_ref[...]).astype(BF16)
    cos = cos_ref[...]
    sin_lo = sa_ref[...]
    sin_hi = sb_ref[...]

    def rope(t):
        return t * cos + pltpu.roll(t, 96, 1) * sin_lo + pltpu.roll(t, 32, 1) * sin_hi

    u = _dot(hn, w_ref[:, 2 * QK_WIDTH + ATTN_WIDTH:IN_WIDTH]) + b_ref[...]
    z_ref[...] = u[:, :CONV_CH] * jax.nn.sigmoid(u[:, CONV_CH:])
    qf = _dot(hn, w_ref[:, 0:QK_WIDTH])
    for g in range(N_HEADS):
        sl = slice(g * 128, (g + 1) * 128)
        q_ref[:, sl] = (rope(qf[:, sl]) * (HEAD_DIM ** -0.5 * LOG2E)).astype(BF16)
    kf = _dot(hn, w_ref[:, QK_WIDTH:2 * QK_WIDTH])
    for g in range(N_HEADS):
        sl = slice(g * 128, (g + 1) * 128)
        k_ref[:, sl] = rope(kf[:, sl]).astype(BF16)
    v_ref[...] = _dot(hn, w_ref[:, 2 * QK_WIDTH:2 * QK_WIDTH + ATTN_WIDTH]).astype(BF16)


def _proj_call(h, g, w, b, cos, sin_lo, sin_hi, l, tm, tiles_per_seq, name):
    t = h.shape[0]
    row = lambda i: (i, 0)
    pos = lambda i: (i % tiles_per_seq, 0)
    half = pl.BlockSpec((tm, 512), row)
    tab = pl.BlockSpec((tm, 128), pos)
    return pl.pallas_call(
        _proj_kernel,
        grid=(t // tm,),
        in_specs=[pl.BlockSpec((tm, D_MODEL), row), _resident((1, D_MODEL)),
                  _layer((D_MODEL, IN_WIDTH), l), _resident((1, 2 * CONV_CH)),
                  tab, tab, tab],
        out_specs=[half, half, half, half],
        out_shape=[jax.ShapeDtypeStruct((t, 512), BF16)] * 3
        + [jax.ShapeDtypeStruct((t, 512), F32)],
        compiler_params=pltpu.CompilerParams(
            dimension_semantics=("parallel",), vmem_limit_bytes=VMEM_LIMIT),
        name=name,
    )(h, g, w, b, cos, sin_lo, sin_hi)


def _component(q, c):
    lane = lax.broadcasted_iota(jnp.int32, q.shape, 1)
    keep = (lane < HEAD_DIM) if c == 0 else (lane >= HEAD_DIM)
    return jnp.where(keep, q, jnp.zeros_like(q))


def _flash_init(m_ref, acc_ref):
    m_ref[...] = jnp.full(m_ref.shape, -jnp.inf, F32)
    acc_ref[...] = jnp.zeros(acc_ref.shape, F32)


def _flash_step(qc, kj, vj, m_ref, acc_ref, mask=None):
    tk = kj.shape[0]
    s = _dot_nt(qc, kj)
    if mask is not None:
        s = jnp.where(mask, s, -jnp.inf)
    m_prev = m_ref[...]
    m_new = jnp.maximum(m_prev, jnp.max(s, axis=-1, keepdims=True))
    alpha = jnp.exp2(m_prev - m_new)
    p = jnp.exp2(s - jnp.concatenate([m_new] * (tk // 128), axis=1))
    v_ones = jnp.concatenate([vj, jnp.ones((tk, 128), BF16)], axis=1)
    acc_ref[...] = (jnp.concatenate([alpha, alpha], axis=1) * acc_ref[...]
                    + _dot(p.astype(BF16), v_ones))
    m_ref[...] = m_new


def _causal_mask(n, nk, shift):
    r = lax.broadcasted_iota(jnp.int32, (n, nk), 0)
    c = lax.broadcasted_iota(jnp.int32, (n, nk), 1)
    return c - r <= shift


def _flash_finish(lam_ref, subg_ref, acc0_ref, acc1_ref):
    lq1, lk1, lq2, lk2 = (lam_ref[i:i + 1, :] for i in range(4))
    lam_init = lam_ref[4:5, 0:1]
    lam = (jnp.exp(jnp.sum(lq1 * lk1, axis=-1, keepdims=True))
           - jnp.exp(jnp.sum(lq2 * lk2, axis=-1, keepdims=True)) + lam_init)
    o = (acc0_ref[:, 0:128] / acc0_ref[:, 128:256]
         - lam * (acc1_ref[:, 0:128] / acc1_ref[:, 128:256]))
    return (_rms(o, subg_ref[...]) * (1.0 - lam_init)).astype(BF16)


def _attn_kernel(lam_ref, subg_ref, q_ref, k_ref, v_ref, km_ref, vm_ref, o_ref,
                 qc_scr, m_scr, acc_scr, *, tq, tk):
    i = pl.program_id(1)
    heads = [slice(h * 128, (h + 1) * 128) for h in range(N_HEADS)]
    chains = [(h, 2 * h + c) for h in range(N_HEADS) for c in range(2)]
    meta_mask = lax.broadcasted_iota(jnp.int32, (tq, META_PAD), 1) < N_META
    for h, n in chains:
        qc_scr[n] = _component(q_ref[:, heads[h]], n % 2)
        _flash_init(m_scr.at[n], acc_scr.at[n])
        _flash_step(qc_scr[n], km_ref[:, heads[h]], vm_ref[:, heads[h]],
                    m_scr.at[n], acc_scr.at[n], mask=meta_mask)

    def kv_block(j, mask):
        off = pl.multiple_of(j * tk, tk)
        for h, n in chains:
            _flash_step(qc_scr[n], k_ref[pl.ds(off, tk), heads[h]],
                        v_ref[pl.ds(off, tk), heads[h]], m_scr.at[n], acc_scr.at[n], mask=mask)

    n_full = (i * tq) // tk

    def body(j, carry):
        for t in range(4):
            kv_block(4 * j + t, None)
        return carry

    lax.fori_loop(0, n_full // 4, body, 0)
    rest = n_full % 4

    @pl.when(rest >= 2)
    def _():
        kv_block(n_full - rest, None)
        kv_block(n_full - rest + 1, None)

    @pl.when(rest % 2 == 1)
    def _():
        kv_block(n_full - 1, None)

    kv_block(n_full, _causal_mask(tq, tk, i * tq - n_full * tk))
    for h in range(N_HEADS):
        o_ref[:, heads[h]] = _flash_finish(lam_ref, subg_ref,
                                           acc_scr.at[2 * h], acc_scr.at[2 * h + 1])


def _attn_call(lam, subg, q, k, v, km, vm, name):
    b, s, _ = q.shape
    tq, tk = Q_TILE, K_TILE
    qblk = pl.BlockSpec((None, tq, ATTN_WIDTH), lambda bi, qi: (bi, qi, 0))
    kvblk = pl.BlockSpec((None, s, ATTN_WIDTH), lambda bi, qi: (bi, 0, 0),
                         pipeline_mode=pl.Buffered(1))
    return pl.pallas_call(
        functools.partial(_attn_kernel, tq=tq, tk=tk),
        grid=(b, s // tq),
        in_specs=[_resident((8, HEAD_DIM)), _resident((1, V_HEAD_DIM)),
                  qblk, kvblk, kvblk,
                  _resident((META_PAD, ATTN_WIDTH)), _resident((META_PAD, ATTN_WIDTH))],
        out_specs=qblk,
        out_shape=jax.ShapeDtypeStruct((b, s, ATTN_WIDTH), BF16),
        scratch_shapes=[pltpu.VMEM((2 * N_HEADS, tq, 128), BF16),
                        pltpu.VMEM((2 * N_HEADS, tq, 128), F32),
                        pltpu.VMEM((2 * N_HEADS, tq, 2 * V_HEAD_DIM), F32)],
        compiler_params=pltpu.CompilerParams(
            dimension_semantics=("parallel", "arbitrary"), vmem_limit_bytes=VMEM_LIMIT),
        name=name,
    )(lam, subg, q, k, v, km, vm)


def _meta_attn_kernel(lam_ref, subg_ref, q_ref, k_ref, v_ref, o_ref, m_scr, acc_scr):
    mask = _causal_mask(N_META, META_PAD, 0)
    for h in range(N_HEADS):
        sl = slice(h * 128, (h + 1) * 128)
        for c in range(2):
            _flash_init(m_scr.at[c], acc_scr.at[c])
            _flash_step(_component(q_ref[:, sl], c), k_ref[:, sl], v_ref[:, sl],
                        m_scr.at[c], acc_scr.at[c], mask=mask)
        o_ref[:, sl] = _flash_finish(lam_ref, subg_ref, acc_scr.at[0], acc_scr.at[1])


def _meta_attn_call(lam, subg, q, k, v, name):
    return pl.pallas_call(
        _meta_attn_kernel,
        out_shape=jax.ShapeDtypeStruct((N_META, ATTN_WIDTH), BF16),
        scratch_shapes=[pltpu.VMEM((2, N_META, 128), F32),
                        pltpu.VMEM((2, N_META, 2 * V_HEAD_DIM), F32)],
        name=name,
    )(lam, subg, q, k, v)


def _conv_kernel(z_ref, halo_ref, prefix_ref, w_ref, b_ref, g_ref, beta_ref, o_ref,
                 sh_scr, *, tm, rows, tiles_per_seq):
    i = pl.program_id(0)
    sh_scr[0, 0:HALO, :] = jnp.where(i % tiles_per_seq == 0, prefix_ref[...], halo_ref[...])
    sh_scr[0, HALO:HALO + tm, :] = z_ref[...]
    span = tm + HALO - SUBLANES
    for r in range(1, SUBLANES):
        sh_scr[r, 0:span, :] = sh_scr[0, r:r + span, :]
    first = HALO - (CONV_WIDTH - 1)
    for c in range(tm // rows):
        acc = jnp.zeros((rows // SUBLANES, SUBLANES, CONV_CH), F32)
        for j in range(CONV_WIDTH):
            a, r = divmod(first + j, SUBLANES)
            tap = sh_scr[r, a * SUBLANES + c * rows:a * SUBLANES + (c + 1) * rows, :]
            acc = acc + w_ref[j] * tap.reshape(rows // SUBLANES, SUBLANES, CONV_CH)
        y = acc.reshape(rows, CONV_CH) + b_ref[...]
        yc = y - jnp.mean(y, axis=-1, keepdims=True)
        var = jnp.mean(yc * yc, axis=-1, keepdims=True)
        y = yc * lax.rsqrt(var + NORM_EPS) * g_ref[...] + beta_ref[...]
        o_ref[c * rows:(c + 1) * rows, :] = (y * jax.nn.sigmoid(y)).astype(BF16)


def _conv_call(z, halo_src, prefix, w, b, g, beta, tm, tiles_per_seq, name):
    t = z.shape[0]
    row = lambda i: (i, 0)
    halo = lambda i: (jnp.maximum(i * (tm // HALO) - 1, 0), 0)
    rows = min(tm, 64)
    return pl.pallas_call(
        functools.partial(_conv_kernel, tm=tm, rows=rows, tiles_per_seq=tiles_per_seq),
        grid=(t // tm,),
        in_specs=[pl.BlockSpec((tm, CONV_CH), row), pl.BlockSpec((HALO, CONV_CH), halo),
                  _resident((HALO, CONV_CH)), _resident((CONV_WIDTH, SUBLANES, CONV_CH)),
                  _resident((1, CONV_CH)), _resident((1, CONV_CH)), _resident((1, CONV_CH))],
        out_specs=pl.BlockSpec((tm, CONV_CH), row),
        out_shape=jax.ShapeDtypeStruct((t, CONV_CH), BF16),
        scratch_shapes=[pltpu.VMEM((SUBLANES, HALO + tm, CONV_CH), F32)],
        compiler_params=pltpu.CompilerParams(
            dimension_semantics=("parallel",), vmem_limit_bytes=VMEM_LIMIT),
        name=name,
    )(z, halo_src, prefix, w, b, g, beta)


def _ffn_kernel(h_ref, a_ref, c_ref, wo_ref, g2_ref, wgu_ref, wd_ref, fg_ref, o_ref,
                hn_scr, act_scr, *, final):
    h1 = (h_ref[...] + _dot(a_ref[...], wo_ref[0:ATTN_WIDTH, :])
          + _dot(c_ref[...], wo_ref[ATTN_WIDTH:ATTN_WIDTH + CONV_CH, :]))
    o_ref[...] = h1
    hn_scr[...] = _rms(h1, g2_ref[...]).astype(BF16)
    for c in range(D_FF // FF_CHUNK):
        sl = slice(c * FF_CHUNK, (c + 1) * FF_CHUNK)
        hn = hn_scr[...]
        gate = _dot(hn, wgu_ref[:, sl])
        up = _dot(hn, wgu_ref[:, D_FF + c * FF_CHUNK:D_FF + (c + 1) * FF_CHUNK])
        act_scr[:, sl] = (gate * jax.nn.sigmoid(gate) * up).astype(BF16)
    h2 = o_ref[...] + _dot(act_scr[...], wd_ref[...])
    if final:
        h2 = _rms(h2, fg_ref[...])
    o_ref[...] = h2


def _ffn_call(h, a, c, wo, g2, wgu, wd, fg, l, tm, final, name):
    t = h.shape[0]
    row = lambda i: (i, 0)
    return pl.pallas_call(
        functools.partial(_ffn_kernel, final=final),
        grid=(t // tm,),
        in_specs=[pl.BlockSpec((tm, D_MODEL), row), pl.BlockSpec((tm, ATTN_WIDTH), row),
                  pl.BlockSpec((tm, CONV_CH), row), _layer((D_MODEL, D_MODEL), l),
                  _resident((1, D_MODEL)), _layer((D_MODEL, 2 * D_FF), l),
                  _layer((D_FF, D_MODEL), l), _resident((1, D_MODEL))],
        out_specs=pl.BlockSpec((tm, D_MODEL), row),
        out_shape=jax.ShapeDtypeStruct((t, D_MODEL), F32),
        scratch_shapes=[pltpu.VMEM((tm, D_MODEL), BF16), pltpu.VMEM((tm, D_FF), BF16)],
        compiler_params=pltpu.CompilerParams(
            dimension_semantics=("parallel",), vmem_limit_bytes=VMEM_LIMIT),
        name=name,
    )(h, a, c, wo, g2, wgu, wd, fg)


def _rope_tables(length):
    pos = jnp.arange(length, dtype=F32)
    inv = ROPE_THETA ** (-jnp.arange(0, HEAD_DIM, 2, dtype=F32) / HEAD_DIM)
    ang = pos[:, None] * inv[None, :]
    cos = jnp.tile(jnp.cos(ang), (1, 4))
    sin = jnp.tile(jnp.sin(ang), (1, 4))
    first_half = (jnp.arange(128) % HEAD_DIM) < HEAD_DIM // 2
    return cos, jnp.where(first_half, -sin, 0.0), jnp.where(first_half, 0.0, sin)


def kernel(x, meta_tokens, norm1_g, w_in, b_glu, conv_w, conv_b, conv_ln_g, conv_ln_b,
           lam_q1, lam_k1, lam_q2, lam_k2, subln_g, w_out, norm2_g, w_gate_up, w_down,
           final_g):
    bsz, seq, d = x.shape
    depth = w_in.shape[0]
    tm = TOKEN_TILE
    tps = seq // tm
    h = x.reshape(bsz * seq, d)
    hm = meta_tokens.astype(x.dtype)

    tables = _rope_tables(N_META + seq)
    tab_meta = [t[:N_META] for t in tables]
    tab_main = [t[N_META:] for t in tables]

    w_in_b = w_in.astype(BF16)
    w_out_b = w_out.astype(BF16)
    w_gu_b = w_gate_up.astype(BF16)
    w_dn_b = w_down.astype(BF16)
    row = lambda a: a.reshape(1, -1)
    zero_halo = jnp.zeros((HALO, CONV_CH), F32)
    fg = row(final_g)
    pad_keys = lambda a: jnp.pad(a, ((0, META_PAD - N_META), (0, 0)))

    for l in range(depth):
        lam_init = 0.8 - 0.6 * math.exp(-0.3 * l)
        last = l == depth - 1
        lam = jnp.concatenate([
            jnp.stack([lam_q1[l], lam_k1[l], lam_q2[l], lam_k2[l]]).astype(F32),
            jnp.full((1, HEAD_DIM), lam_init, F32), jnp.zeros((3, HEAD_DIM), F32)])
        subg = row(subln_g[l])
        pargs = (row(norm1_g[l]), w_in_b, row(b_glu[l]))
        q, k, v, z = _proj_call(h, *pargs, *tab_main, l, PROJ_TILE, seq // PROJ_TILE,
                                f"proj{l}")
        qm, km, vm, zm = _proj_call(hm, *pargs, *tab_meta, l, N_META, 1, f"proj_meta{l}")
        km, vm = pad_keys(km), pad_keys(vm)

        attn = _attn_call(lam, subg, q.reshape(bsz, seq, -1), k.reshape(bsz, seq, -1),
                          v.reshape(bsz, seq, -1), km, vm, f"attn{l}")
        attn = attn.reshape(bsz * seq, -1)

        conv_w_rep = jnp.broadcast_to(conv_w[l][:, None, :], (CONV_WIDTH, SUBLANES, CONV_CH))
        cargs = (conv_w_rep, row(conv_b[l]), row(conv_ln_g[l]), row(conv_ln_b[l]))
        prefix = jnp.concatenate([jnp.zeros((HALO - N_META, CONV_CH), F32), zm], axis=0)
        conv = _conv_call(z, z, prefix, *cargs, tm, tps, f"conv{l}")

        fargs = (w_out_b, row(norm2_g[l]), w_gu_b, w_dn_b, fg, l)
        h = _ffn_call(h, attn, conv, *fargs, tm, last, f"ffn{l}")
        if not last:
            attn_m = _meta_attn_call(lam, subg, qm, km, vm, f"attn_meta{l}")
            conv_m = _conv_call(zm, zero_halo, zero_halo, *cargs, N_META, 1, f"conv_meta{l}")
            hm = _ffn_call(hm, attn_m, conv_m, *fargs, N_META, False, f"ffn_meta{l}")
    return h.reshape(bsz, seq, d)
```

```python
import functools
import math

import jax
import jax.numpy as jnp
from jax import lax
from jax.experimental import pallas as pl
from jax.experimental.pallas import tpu as pltpu

D_MODEL = 1024
N_META = 16
ATTN_WIDTH = 512
CONV_CH = 512
N_HEADS = 4
V_HEAD_DIM = 128
HEAD_DIM = 64
QK_WIDTH = 512
IN_WIDTH = 2 * QK_WIDTH + ATTN_WIDTH + 2 * CONV_CH
CONV_WIDTH = 31
D_FF = 2816
ROPE_THETA = 10000.0
NORM_EPS = 1e-5
LOG2E = math.log2(math.e)
SUBLANES = 8

HALO = 32
META_PAD = 128
TOKEN_TILE = 512
PROJ_TILE = 1024
Q_TILE = 512
K_TILE = 512
FF_CHUNK = 256
VMEM_LIMIT = 56 * 1024 * 1024

F32 = jnp.float32
BF16 = jnp.bfloat16


def _dot(a, b):
    return jnp.dot(a, b, preferred_element_type=F32)


def _dot_nt(a, b):
    return lax.dot_general(a, b, (((1,), (1,)), ((), ())), preferred_element_type=F32)


def _rms(x, g):
    ms = jnp.mean(x * x, axis=-1, keepdims=True)
    return x * lax.rsqrt(ms + NORM_EPS) * g


def _resident(shape):
    zeros = (0,) * len(shape)
    return pl.BlockSpec(shape, lambda *_: zeros, pipeline_mode=pl.Buffered(1))


def _layer(shape, l):
    index = (l,) + (0,) * len(shape)
    return pl.BlockSpec((None,) + shape, lambda *_: index, pipeline_mode=pl.Buffered(1))


def _proj_kernel(h_ref, g_ref, w_ref, b_ref, cos_ref, sa_ref, sb_ref,
                 q_ref, k_ref, v_ref, z_ref):
    hn = _rms(h_ref[...], g_ref[...]).astype(BF16)
    cos = cos_ref[...]
    sin_lo = sa_ref[...]
    sin_hi = sb_ref[...]

    def rope(t):
        return t * cos + pltpu.roll(t, 96, 1) * sin_lo + pltpu.roll(t, 32, 1) * sin_hi

    u = _dot(hn, w_ref[:, 2 * QK_WIDTH + ATTN_WIDTH:IN_WIDTH]) + b_ref[...]
    z_ref[...] = u[:, :CONV_CH] * jax.nn.sigmoid(u[:, CONV_CH:])
    qf = _dot(hn, w_ref[:, 0:QK_WIDTH])
    for g in range(N_HEADS):
        sl = slice(g * 128, (g + 1) * 128)
        q_ref[:, sl] = (rope(qf[:, sl]) * (HEAD_DIM ** -0.5 * LOG2E)).astype(BF16)
    kf = _dot(hn, w_ref[:, QK_WIDTH:2 * QK_WIDTH])
    for g in range(N_HEADS):
        sl = slice(g * 128, (g + 1) * 128)
        k_ref[:, sl] = rope(kf[:, sl]).astype(BF16)
    v_ref[...] = _dot(hn, w_ref[:, 2 * QK_WIDTH:2 * QK_WIDTH + ATTN_WIDTH]).astype(BF16)


def _proj_call(h, g, w, b, cos, sin_lo, sin_hi, l, tm, tiles_per_seq, name):
    t = h.shape[0]
    row = lambda i: (i, 0)
    pos = lambda i: (i % tiles_per_seq, 0)
    half = pl.BlockSpec((tm, 512), row)
    tab = pl.BlockSpec((tm, 128), pos)
    return pl.pallas_call(
        _proj_kernel,
        grid=(t // tm,),
        in_specs=[pl.BlockSpec((tm, D_MODEL), row), _resident((1, D_MODEL)),
                  _layer((D_MODEL, IN_WIDTH), l), _resident((1, 2 * CONV_CH)),
                  tab, tab, tab],
        out_specs=[half, half, half, half],
        out_shape=[jax.ShapeDtypeStruct((t, 512), BF16)] * 3
        + [jax.ShapeDtypeStruct((t, 512), F32)],
        compiler_params=pltpu.CompilerParams(
            dimension_semantics=("parallel",), vmem_limit_bytes=VMEM_LIMIT),
        name=name,
    )(h, g, w, b, cos, sin_lo, sin_hi)


def _component(q, c):
    lane = lax.broadcasted_iota(jnp.int32, q.shape, 1)
    keep = (lane < HEAD_DIM) if c == 0 else (lane >= HEAD_DIM)
    return jnp.where(keep, q, jnp.zeros_like(q))


def _flash_init(m_ref, acc_ref):
    m_ref[...] = jnp.full(m_ref.shape, -jnp.inf, F32)
    acc_ref[...] = jnp.zeros(acc_ref.shape, F32)


def _flash_step(qc, kj, vj, m_ref, acc_ref, mask=None):
    tk = kj.shape[0]
    s = _dot_nt(qc, kj)
    if mask is not None:
        s = jnp.where(mask, s, -jnp.inf)
    m_prev = m_ref[...]
    m_new = jnp.maximum(m_prev, jnp.max(s, axis=-1, keepdims=True))
    alpha = jnp.exp2(m_prev - m_new)
    p = jnp.exp2((s - jnp.concatenate([m_new] * (tk // 128), axis=1)).astype(BF16))
    v_ones = jnp.concatenate([vj, jnp.ones((tk, 128), BF16)], axis=1)
    acc_ref[...] = (jnp.concatenate([alpha, alpha], axis=1) * acc_ref[...]
                    + _dot(p, v_ones))
    m_ref[...] = m_new


def _causal_mask(n, nk, shift):
    r = lax.broadcasted_iota(jnp.int32, (n, nk), 0)
    c = lax.broadcasted_iota(jnp.int32, (n, nk), 1)
    return c - r <= shift


def _flash_finish(lam_ref, subg_ref, acc0_ref, acc1_ref):
    lq1, lk1, lq2, lk2 = (lam_ref[i:i + 1, :] for i in range(4))
    lam_init = lam_ref[4:5, 0:1]
    lam = (jnp.exp(jnp.sum(lq1 * lk1, axis=-1, keepdims=True))
           - jnp.exp(jnp.sum(lq2 * lk2, axis=-1, keepdims=True)) + lam_init)
    o = (acc0_ref[:, 0:128] / acc0_ref[:, 128:256]
         - lam * (acc1_ref[:, 0:128] / acc1_ref[:, 128:256]))
    return (_rms(o, subg_ref[...]) * (1.0 - lam_init)).astype(BF16)


def _attn_kernel(lam_ref, subg_ref, q_ref, k_ref, v_ref, km_ref, vm_ref, o_ref,
                 qc_scr, m_scr, acc_scr, *, tq, tk):
    i = pl.program_id(1)
    heads = [slice(h * 128, (h + 1) * 128) for h in range(N_HEADS)]
    chains = [(h, 2 * h + c) for h in range(N_HEADS) for c in range(2)]
    meta_mask = lax.broadcasted_iota(jnp.int32, (tq, META_PAD), 1) < N_META
    for h, n in chains:
        qc_scr[n] = _component(q_ref[:, heads[h]], n % 2)
        _flash_init(m_scr.at[n], acc_scr.at[n])
        _flash_step(qc_scr[n], km_ref[:, heads[h]], vm_ref[:, heads[h]],
                    m_scr.at[n], acc_scr.at[n], mask=meta_mask)

    def kv_block(j, mask):
        off = pl.multiple_of(j * tk, tk)
        for h, n in chains:
            _flash_step(qc_scr[n], k_ref[pl.ds(off, tk), heads[h]],
                        v_ref[pl.ds(off, tk), heads[h]], m_scr.at[n], acc_scr.at[n], mask=mask)

    n_full = (i * tq) // tk

    def body(j, carry):
        for t in range(4):
            kv_block(4 * j + t, None)
        return carry

    lax.fori_loop(0, n_full // 4, body, 0)
    rest = n_full % 4

    @pl.when(rest >= 2)
    def _():
        kv_block(n_full - rest, None)
        kv_block(n_full - rest + 1, None)

    @pl.when(rest % 2 == 1)
    def _():
        kv_block(n_full - 1, None)

    kv_block(n_full, _causal_mask(tq, tk, i * tq - n_full * tk))
    for h in range(N_HEADS):
        o_ref[:, heads[h]] = _flash_finish(lam_ref, subg_ref,
                                           acc_scr.at[2 * h], acc_scr.at[2 * h + 1])


def _attn_call(lam, subg, q, k, v, km, vm, name):
    b, s, _ = q.shape
    tq, tk = Q_TILE, K_TILE
    qblk = pl.BlockSpec((None, tq, ATTN_WIDTH), lambda bi, qi: (bi, qi, 0))
    kvblk = pl.BlockSpec((None, s, ATTN_WIDTH), lambda bi, qi: (bi, 0, 0),
                         pipeline_mode=pl.Buffered(1))
    return pl.pallas_call(
        functools.partial(_attn_kernel, tq=tq, tk=tk),
        grid=(b, s // tq),
        in_specs=[_resident((8, HEAD_DIM)), _resident((1, V_HEAD_DIM)),
                  qblk, kvblk, kvblk,
                  _resident((META_PAD, ATTN_WIDTH)), _resident((META_PAD, ATTN_WIDTH))],
        out_specs=qblk,
        out_shape=jax.ShapeDtypeStruct((b, s, ATTN_WIDTH), BF16),
        scratch_shapes=[pltpu.VMEM((2 * N_HEADS, tq, 128), BF16),
                        pltpu.VMEM((2 * N_HEADS, tq, 128), F32),
                        pltpu.VMEM((2 * N_HEADS, tq, 2 * V_HEAD_DIM), F32)],
        compiler_params=pltpu.CompilerParams(
            dimension_semantics=("parallel", "arbitrary"), vmem_limit_bytes=VMEM_LIMIT),
        name=name,
    )(lam, subg, q, k, v, km, vm)


def _meta_attn_kernel(lam_ref, subg_ref, q_ref, k_ref, v_ref, o_ref, m_scr, acc_scr):
    mask = _causal_mask(N_META, META_PAD, 0)
    for h in range(N_HEADS):
        sl = slice(h * 128, (h + 1) * 128)
        for c in range(2):
            _flash_init(m_scr.at[c], acc_scr.at[c])
            _flash_step(_component(q_ref[:, sl], c), k_ref[:, sl], v_ref[:, sl],
                        m_scr.at[c], acc_scr.at[c], mask=mask)
        o_ref[:, sl] = _flash_finish(lam_ref, subg_ref, acc_scr.at[0], acc_scr.at[1])


def _meta_attn_call(lam, subg, q, k, v, name):
    return pl.pallas_call(
        _meta_attn_kernel,
        out_shape=jax.ShapeDtypeStruct((N_META, ATTN_WIDTH), BF16),
        scratch_shapes=[pltpu.VMEM((2, N_META, 128), F32),
                        pltpu.VMEM((2, N_META, 2 * V_HEAD_DIM), F32)],
        name=name,
    )(lam, subg, q, k, v)


def _conv_kernel(z_ref, halo_ref, prefix_ref, w_ref, b_ref, g_ref, beta_ref, o_ref,
                 sh_scr, *, tm, rows, tiles_per_seq):
    i = pl.program_id(0)
    sh_scr[0, 0:HALO, :] = jnp.where(i % tiles_per_seq == 0, prefix_ref[...], halo_ref[...])
    sh_scr[0, HALO:HALO + tm, :] = z_ref[...]
    span = tm + HALO - SUBLANES
    for r in range(1, SUBLANES):
        sh_scr[r, 0:span, :] = sh_scr[0, r:r + span, :]
    first = HALO - (CONV_WIDTH - 1)
    for c in range(tm // rows):
        acc = jnp.zeros((rows // SUBLANES, SUBLANES, CONV_CH), F32)
        for j in range(CONV_WIDTH):
            a, r = divmod(first + j, SUBLANES)
            tap = sh_scr[r, a * SUBLANES + c * rows:a * SUBLANES + (c + 1) * rows, :]
            acc = acc + w_ref[j] * tap.reshape(rows // SUBLANES, SUBLANES, CONV_CH)
        y = acc.reshape(rows, CONV_CH) + b_ref[...]
        yc = y - jnp.mean(y, axis=-1, keepdims=True)
        var = jnp.mean(yc * yc, axis=-1, keepdims=True)
        y = yc * lax.rsqrt(var + NORM_EPS) * g_ref[...] + beta_ref[...]
        o_ref[c * rows:(c + 1) * rows, :] = (y * jax.nn.sigmoid(y)).astype(BF16)


def _conv_call(z, halo_src, prefix, w, b, g, beta, tm, tiles_per_seq, name):
    t = z.shape[0]
    row = lambda i: (i, 0)
    halo = lambda i: (jnp.maximum(i * (tm // HALO) - 1, 0), 0)
    rows = min(tm, 64)
    return pl.pallas_call(
        functools.partial(_conv_kernel, tm=tm, rows=rows, tiles_per_seq=tiles_per_seq),
        grid=(t // tm,),
        in_specs=[pl.BlockSpec((tm, CONV_CH), row), pl.BlockSpec((HALO, CONV_CH), halo),
                  _resident((HALO, CONV_CH)), _resident((CONV_WIDTH, SUBLANES, CONV_CH)),
                  _resident((1, CONV_CH)), _resident((1, CONV_CH)), _resident((1, CONV_CH))],
        out_specs=pl.BlockSpec((tm, CONV_CH), row),
        out_shape=jax.ShapeDtypeStruct((t, CONV_CH), BF16),
        scratch_shapes=[pltpu.VMEM((SUBLANES, HALO + tm, CONV_CH), F32)],
        compiler_params=pltpu.CompilerParams(
            dimension_semantics=("parallel",), vmem_limit_bytes=VMEM_LIMIT),
        name=name,
    )(z, halo_src, prefix, w, b, g, beta)


def _ffn_kernel(h_ref, a_ref, c_ref, wo_ref, g2_ref, wgu_ref, wd_ref, fg_ref, o_ref,
                hn_scr, act_scr, *, final):
    h1 = (h_ref[...] + _dot(a_ref[...], wo_ref[0:ATTN_WIDTH, :])
          + _dot(c_ref[...], wo_ref[ATTN_WIDTH:ATTN_WIDTH + CONV_CH, :]))
    o_ref[...] = h1
    hn_scr[...] = _rms(h1, g2_ref[...]).astype(BF16)
    for c in range(D_FF // FF_CHUNK):
        sl = slice(c * FF_CHUNK, (c + 1) * FF_CHUNK)
        hn = hn_scr[...]
        gate = _dot(hn, wgu_ref[:, sl])
        up = _dot(hn, wgu_ref[:, D_FF + c * FF_CHUNK:D_FF + (c + 1) * FF_CHUNK])
        act_scr[:, sl] = (gate * jax.nn.sigmoid(gate) * up).astype(BF16)
    h2 = o_ref[...] + _dot(act_scr[...], wd_ref[...])
    if final:
        h2 = _rms(h2, fg_ref[...])
    o_ref[...] = h2


def _ffn_call(h, a, c, wo, g2, wgu, wd, fg, l, tm, final, name):
    t = h.shape[0]
    row = lambda i: (i, 0)
    return pl.pallas_call(
        functools.partial(_ffn_kernel, final=final),
        grid=(t // tm,),
        in_specs=[pl.BlockSpec((tm, D_MODEL), row), pl.BlockSpec((tm, ATTN_WIDTH), row),
                  pl.BlockSpec((tm, CONV_CH), row), _layer((D_MODEL, D_MODEL), l),
                  _resident((1, D_MODEL)), _layer((D_MODEL, 2 * D_FF), l),
                  _layer((D_FF, D_MODEL), l), _resident((1, D_MODEL))],
        out_specs=pl.BlockSpec((tm, D_MODEL), row),
        out_shape=jax.ShapeDtypeStruct((t, D_MODEL), F32),
        scratch_shapes=[pltpu.VMEM((tm, D_MODEL), BF16), pltpu.VMEM((tm, D_FF), BF16)],
        compiler_params=pltpu.CompilerParams(
            dimension_semantics=("parallel",), vmem_limit_bytes=VMEM_LIMIT),
        name=name,
    )(h, a, c, wo, g2, wgu, wd, fg)


def _rope_tables(length):
    pos = jnp.arange(length, dtype=F32)
    inv = ROPE_THETA ** (-jnp.arange(0, HEAD_DIM, 2, dtype=F32) / HEAD_DIM)
    ang = pos[:, None] * inv[None, :]
    cos = jnp.tile(jnp.cos(ang), (1, 4))
    sin = jnp.tile(jnp.sin(ang), (1, 4))
    first_half = (jnp.arange(128) % HEAD_DIM) < HEAD_DIM // 2
    return cos, jnp.where(first_half, -sin, 0.0), jnp.where(first_half, 0.0, sin)


def kernel(x, meta_tokens, norm1_g, w_in, b_glu, conv_w, conv_b, conv_ln_g, conv_ln_b,
           lam_q1, lam_k1, lam_q2, lam_k2, subln_g, w_out, norm2_g, w_gate_up, w_down,
           final_g):
    bsz, seq, d = x.shape
    depth = w_in.shape[0]
    tm = TOKEN_TILE
    tps = seq // tm
    h = x.reshape(bsz * seq, d)
    hm = meta_tokens.astype(x.dtype)

    tables = _rope_tables(N_META + seq)
    tab_meta = [t[:N_META] for t in tables]
    tab_main = [t[N_META:] for t in tables]

    w_in_b = w_in.astype(BF16)
    w_out_b = w_out.astype(BF16)
    w_gu_b = w_gate_up.astype(BF16)
    w_dn_b = w_down.astype(BF16)
    row = lambda a: a.reshape(1, -1)
    zero_halo = jnp.zeros((HALO, CONV_CH), F32)
    fg = row(final_g)
    pad_keys = lambda a: jnp.pad(a, ((0, META_PAD - N_META), (0, 0)))

    for l in range(depth):
        lam_init = 0.8 - 0.6 * math.exp(-0.3 * l)
        last = l == depth - 1
        lam = jnp.concatenate([
            jnp.stack([lam_q1[l], lam_k1[l], lam_q2[l], lam_k2[l]]).astype(F32),
            jnp.full((1, HEAD_DIM), lam_init, F32), jnp.zeros((3, HEAD_DIM), F32)])
        subg = row(subln_g[l])
        pargs = (row(norm1_g[l]), w_in_b, row(b_glu[l]))
        q, k, v, z = _proj_call(h, *pargs, *tab_main, l, PROJ_TILE, seq // PROJ_TILE,
                                f"proj{l}")
        qm, km, vm, zm = _proj_call(hm, *pargs, *tab_meta, l, N_META, 1, f"proj_meta{l}")
        km, vm = pad_keys(km), pad_keys(vm)

        attn = _attn_call(lam, subg, q.reshape(bsz, seq, -1), k.reshape(bsz, seq, -1),
                          v.reshape(bsz, seq, -1), km, vm, f"attn{l}")
        attn = attn.reshape(bsz * seq, -1)

        conv_w_rep = jnp.broadcast_to(conv_w[l][:, None, :], (CONV_WIDTH, SUBLANES, CONV_CH))
        cargs = (conv_w_rep, row(conv_b[l]), row(conv_ln_g[l]), row(conv_ln_b[l]))
        prefix = jnp.concatenate([jnp.zeros((HALO - N_META, CONV_CH), F32), zm], axis=0)
        conv = _conv_call(z, z, prefix, *cargs, tm, tps, f"conv{l}")

        fargs = (w_out_b, row(norm2_g[l]), w_gu_b, w_dn_b, fg, l)
        h = _ffn_call(h, attn, conv, *fargs, tm, last, f"ffn{l}")
        if not last:
            attn_m = _meta_attn_call(lam, subg, qm, km, vm, f"attn_meta{l}")
            conv_m = _conv_call(zm, zero_halo, zero_halo, *cargs, N_META, 1, f"conv_meta{l}")
            hm = _ffn_call(hm, attn_m, conv_m, *fargs, N_META, False, f"ffn_meta{l}")
    return h.reshape(bsz, seq, d)
```
